```python
import jax, jax.numpy as jnp
from jax import lax
import numpy as np

D_MODEL = 4096
BATCH = 1
SEQ = 8192
DEPTH = 1

EPS = 1e-6
POOL_WIDTH = D_MODEL // 2
POOL_WINDOWS = (2, 4, 8, 16)
N_POOL_GROUPS = len(POOL_WINDOWS)
POOL_GROUP = POOL_WIDTH // N_POOL_GROUPS
HGRN_WIDTH = D_MODEL // 2
HGRN_HEAD_DIM = 128
HGRN_HEADS = HGRN_WIDTH // HGRN_HEAD_DIM
CHUNK = 64
D_FF = -(-8 * D_MODEL // (3 * 256)) * 256
N_IN = POOL_WIDTH + 4 * HGRN_WIDTH + 2 * D_MODEL
SPLITS = (POOL_WIDTH,
          POOL_WIDTH + HGRN_WIDTH,
          POOL_WIDTH + 2 * HGRN_WIDTH,
          POOL_WIDTH + 3 * HGRN_WIDTH,
          POOL_WIDTH + 4 * HGRN_WIDTH,
          POOL_WIDTH + 4 * HGRN_WIDTH + D_MODEL)

kernel_name = "hybrid_pool_hgrn2_gated_block"


def rms_norm(x, w):
    xf = x.astype(jnp.float32)
    y = xf * lax.rsqrt(jnp.mean(xf * xf, axis=-1, keepdims=True) + EPS)
    return (y * w.astype(jnp.float32)).astype(x.dtype)


def causal_multiscale_pool(z):
    B, T, _ = z.shape
    zf = z.astype(jnp.float32).reshape(B, T, N_POOL_GROUPS, POOL_GROUP)
    cs = jnp.concatenate([jnp.zeros((B, 1, N_POOL_GROUPS, POOL_GROUP), jnp.float32),
                          jnp.cumsum(zf, axis=1)], axis=1)
    t = jnp.arange(T)
    outs = []
    for gi, w in enumerate(POOL_WINDOWS):
        start = jnp.maximum(t + 1 - w, 0)
        window_sum = cs[:, 1:, gi] - cs[:, start, gi]
        count = (t + 1 - start).astype(jnp.float32)[None, :, None]
        outs.append(window_sum / count)
    pooled = jnp.stack(outs, axis=2)
    return pooled - zf


def hgrn2_chunkwise(q, k, v, log_f):
    B, T, H, DK = q.shape
    DV = v.shape[-1]
    n = T // CHUNK

    def to_chunks(a):
        return a.reshape(B, n, CHUNK, H, a.shape[-1]).transpose(1, 0, 3, 2, 4)

    qc, kc, vc, gc = to_chunks(q), to_chunks(k), to_chunks(v), to_chunks(log_f)
    causal = jnp.tril(jnp.ones((CHUNK, CHUNK), dtype=bool))[:, :, None]

    def step(S, inp):
        qb, kb, vb, gb = inp
        b = jnp.cumsum(gb, axis=2)
        diff = b[:, :, :, None, :] - b[:, :, None, :, :]
        decay = jnp.exp(jnp.where(causal, diff, -jnp.inf))
        scores = jnp.einsum('bhtk,bhtsk,bhsk->bhts', qb, decay, kb)
        o_intra = jnp.einsum('bhts,bhsv->bhtv', scores, vb)
        o_inter = jnp.einsum('bhtk,bhkv->bhtv', qb * jnp.exp(b), S)
        b_end = b[:, :, -1:, :]
        k_dec = kb * jnp.exp(b_end - b)
        S_new = jnp.exp(b_end[:, :, 0, :])[..., None] * S + jnp.einsum('bhsk,bhsv->bhkv', k_dec, vb)
        return S_new, o_intra + o_inter

    S0 = jnp.zeros((B, H, DK, DV), jnp.float32)
    _, ys = lax.scan(step, S0, (qc, kc, vc, gc))
    return ys.transpose(1, 0, 3, 2, 4).reshape(B, T, H, DV)


def setup_inputs(seed: int = 0) -> dict:
    key = jax.random.key(seed)
    ks = jax.random.split(key, 16)
    f32 = jnp.float32

    def dense(k, shape, fan_in):
        return jax.random.normal(k, shape, f32) * (fan_in ** -0.5)

    def gain(k, shape):
        return 1.0 + 0.02 * jax.random.normal(k, shape, f32)

    L = DEPTH
    return {
        "x": jax.random.normal(ks[0], (BATCH, SEQ, D_MODEL), f32),
        "g_mix": gain(ks[1], (L, D_MODEL)),
        "w_in": dense(ks[2], (L, D_MODEL, N_IN), D_MODEL),
        "w_pool_group": dense(ks[3], (L, N_POOL_GROUPS, POOL_GROUP, POOL_GROUP), POOL_GROUP),
        "pool_scale": gain(ks[4], (L, POOL_WIDTH)),
        "lb_param": 0.5 * jax.random.normal(ks[5], (L + 1, HGRN_WIDTH), f32),
        "hgrn_norm": gain(ks[6], (L, HGRN_WIDTH)),
        "w_up_pool": dense(ks[7], (L, POOL_WIDTH, D_MODEL), POOL_WIDTH),
        "w_up_hgrn": dense(ks[8], (L, HGRN_WIDTH, D_MODEL), HGRN_WIDTH),
        "w_out": dense(ks[9], (L, D_MODEL, D_MODEL), D_MODEL),
        "g_ffn": gain(ks[10], (L, D_MODEL)),
        "w_ffn_gate": dense(ks[11], (L, D_MODEL, D_FF), D_MODEL),
        "w_ffn_up": dense(ks[12], (L, D_MODEL, D_FF), D_MODEL),
        "w_ffn_down": dense(ks[13], (L, D_FF, D_MODEL), D_FF),
        "g_final": gain(ks[14], (D_MODEL,)),
    }


def reference(x, g_mix, w_in, w_pool_group, pool_scale, lb_param, hgrn_norm,
              w_up_pool, w_up_hgrn, w_out, g_ffn, w_ffn_gate, w_ffn_up, w_ffn_down, g_final):
    B, T, _ = x.shape
    f32 = jnp.float32
    lower_bounds = jnp.cumsum(jax.nn.softmax(lb_param.astype(f32), axis=0), axis=0)

    def heads(a):
        return a.reshape(B, T, HGRN_HEADS, HGRN_HEAD_DIM)

    h = x
    for l in range(DEPTH):
        u = rms_norm(h, g_mix[l])
        proj = u @ w_in[l]
        z_pool, q, f_logit, i_in, o_gate, gate_a, gate_b = jnp.split(proj, SPLITS, axis=-1)

        pooled = causal_multiscale_pool(z_pool).astype(x.dtype)
        y_pool = jnp.einsum('btng,nge->btne', pooled, w_pool_group[l])
        y_pool = (y_pool * pool_scale[l].reshape(N_POOL_GROUPS, POOL_GROUP)).reshape(B, T, POOL_WIDTH)

        lb = lower_bounds[l]
        f = lb + (1.0 - lb) * jax.nn.sigmoid(f_logit.astype(f32))
        k = 1.0 - f
        log_f = jnp.log(f)
        qh = jax.nn.silu(q.astype(f32))
        o = hgrn2_chunkwise(heads(qh), heads(k), heads(i_in.astype(f32)), heads(log_f))
        o = rms_norm(o, hgrn_norm[l].reshape(HGRN_HEADS, HGRN_HEAD_DIM))
        y_hgrn = (o.reshape(B, T, HGRN_WIDTH) * jax.nn.silu(o_gate.astype(f32))).astype(x.dtype)

        merged = (jax.nn.sigmoid(gate_a) * (y_pool @ w_up_pool[l])
                  + jax.nn.sigmoid(gate_b) * (y_hgrn @ w_up_hgrn[l]))
        h = h + merged @ w_out[l]

        v = rms_norm(h, g_ffn[l])
        h = h + (jax.nn.silu(v @ w_ffn_gate[l]) * (v @ w_ffn_up[l])) @ w_ffn_down[l]

    return rms_norm(h, g_final)
```

```python
import functools

import numpy as np
import jax
import jax.numpy as jnp
from jax import lax
from jax.experimental import pallas as pl
from jax.experimental.pallas import tpu as pltpu

EPS = 1e-6
POOL_WINDOWS = (2, 4, 8, 16)
POOL_HALO = 16
HEAD_DIM = 128
CHUNK = 64
LEVELS = (32, 16, 8, 4, 2)

F32 = jnp.float32
BF16 = jnp.bfloat16

VMEM_LIMIT_BYTES = 60 * 1024 * 1024


def _tile(dim, pref, align=128):
    if dim <= pref:
        return dim
    t = (pref // align) * align
    while t >= align:
        if dim % t == 0:
            return t
        t -= align
    return dim


def _params(sem):
    return pltpu.CompilerParams(dimension_semantics=sem, vmem_limit_bytes=VMEM_LIMIT_BYTES)


def _sigmoid(x):
    return 1.0 / (1.0 + jnp.exp(-x))


def _rms_scale(h):
    return lax.rsqrt(jnp.mean(h * h, axis=-1, keepdims=True) + EPS)


def _rmsnorm_kernel(x_ref, g_ref, o_ref):
    x = x_ref[...]
    o_ref[...] = (x * _rms_scale(x) * g_ref[...]).astype(o_ref.dtype)


def _rmsnorm(x, g, out_dtype):
    t, d = x.shape
    tm = _tile(t, 256, 8)
    return pl.pallas_call(
        _rmsnorm_kernel,
        out_shape=jax.ShapeDtypeStruct((t, d), out_dtype),
        grid=(t // tm,),
        in_specs=[pl.BlockSpec((tm, d), lambda i: (i, 0)),
                  pl.BlockSpec((1, d), lambda i: (0, 0))],
        out_specs=pl.BlockSpec((tm, d), lambda i: (i, 0)),
        compiler_params=_params(("parallel",)),
        name="rmsnorm",
    )(x, g.reshape(1, d))


def _proj_kernel(u_ref, w_ref, lbp_ref, o_ref, *, layer, tn, half):
    seg = (pl.program_id(1) * tn) // half
    acc = jnp.dot(u_ref[...], w_ref[...], preferred_element_type=F32)

    @pl.when(jnp.logical_or(seg == 0, seg == 3))
    def _():
        o_ref[...] = acc

    @pl.when(jnp.logical_or(seg == 1, seg == 4))
    def _():
        o_ref[...] = acc * _sigmoid(acc)

    @pl.when(seg == 2)
    def _():
        p = lbp_ref[...]
        e = jnp.exp(p - jnp.max(p, axis=0, keepdims=True))
        lb = jnp.sum(e[: layer + 1], axis=0, keepdims=True) / jnp.sum(e, axis=0, keepdims=True)
        o_ref[...] = lb + (1.0 - lb) * _sigmoid(acc)

    @pl.when(seg >= 5)
    def _():
        o_ref[...] = _sigmoid(acc)


def _proj(u, w, lb_param, layer):
    t, d = u.shape
    n_in = w.shape[1]
    half = d // 2
    tm = _tile(t, 1024, 8)
    tn = _tile(half, 1024)
    nb = half // tn

    def lbp_map(i, j):
        return (0, jnp.clip(j - 2 * nb, 0, nb - 1))

    return pl.pallas_call(
        functools.partial(_proj_kernel, layer=layer, tn=tn, half=half),
        out_shape=jax.ShapeDtypeStruct((t, n_in), F32),
        grid=(t // tm, n_in // tn),
        in_specs=[pl.BlockSpec((tm, d), lambda i, j: (i, 0)),
                  pl.BlockSpec((d, tn), lambda i, j: (0, j)),
                  pl.BlockSpec((lb_param.shape[0], tn), lbp_map)],
        out_specs=pl.BlockSpec((tm, tn), lambda i, j: (i, j)),
        compiler_params=_params(("parallel", "arbitrary")),
        name="in_proj",
    )(u, w, lb_param)


def _pool_kernel(z_ref, halo_ref, w_ref, s_ref, o_ref, *, tp):
    i = pl.program_id(0)
    g = pl.program_id(1)
    z = z_ref[...]
    halo = jnp.where(i > 0, halo_ref[...], 0.0)
    full = jnp.concatenate([halo, z], axis=0)
    t = i * tp + lax.broadcasted_iota(jnp.int32, (tp, 1), 0)

    for gi, w in enumerate(POOL_WINDOWS):
        @pl.when(g == gi)
        def _(w=w):
            s = full
            k = 1
            while k < w:
                s = s + pltpu.roll(s, k, 0)
                k *= 2
            count = jnp.minimum(t + 1, w).astype(F32)
            pooled = s[POOL_HALO:] / count - z
            y = jnp.dot(pooled.astype(BF16), w_ref[0], preferred_element_type=F32)
            o_ref[...] = (y * s_ref[0]).astype(o_ref.dtype)


def _pool(act, w_group, scale):
    t = act.shape[0]
    ng, gsz, _ = w_group.shape
    assert ng == len(POOL_WINDOWS)
    tp = _tile(t, 512, POOL_HALO)
    hb = tp // POOL_HALO
    return pl.pallas_call(
        functools.partial(_pool_kernel, tp=tp),
        out_shape=jax.ShapeDtypeStruct((t, ng * gsz), BF16),
        grid=(t // tp, ng),
        in_specs=[pl.BlockSpec((tp, gsz), lambda i, g: (i, g)),
                  pl.BlockSpec((POOL_HALO, gsz), lambda i, g: (jnp.maximum(i * hb - 1, 0), g)),
                  pl.BlockSpec((1, gsz, gsz), lambda i, g: (g, 0, 0)),
                  pl.BlockSpec((1, 1, gsz), lambda i, g: (g, 0, 0))],
        out_specs=pl.BlockSpec((tp, gsz), lambda i, g: (i, g)),
        compiler_params=_params(("parallel", "arbitrary")),
        name="pool_mixer",
    )(act, act, w_group, scale.reshape(ng, 1, gsz))


def _level_ids():
    t = np.arange(CHUNK)[:, None]
    s = np.arange(CHUNK)[None, :]
    ids = -np.ones((CHUNK, CHUNK), np.int32)
    ids[t == s] = 0
    for lid, n in enumerate((1,) + tuple(reversed(LEVELS)), start=1):
        m = (t // (2 * n) == s // (2 * n)) & (t % (2 * n) >= n) & (s % (2 * n) < n)
        ids[m] = lid
    return ids


def _nt_dot(a, b):
    return lax.dot_general(a, b, (((1,), (1,)), ((), ())), preferred_element_type=F32)


def _tn_dot(a, b):
    return lax.dot_general(a, b, (((0,), (0,)), ((), ())), preferred_element_type=F32)


def _hgrn_kernel(q_ref, f_ref, v_ref, og_ref, gn_ref, ids_ref, o_ref, st_ref, b_ref, *, nh, nchunk):
    @pl.when(pl.program_id(1) == 0)
    def _():
        st_ref[...] = jnp.zeros_like(st_ref)

    ids = ids_ref[...]
    row = lax.broadcasted_iota(jnp.int32, (CHUNK, HEAD_DIM), 0)
    sub = lax.broadcasted_iota(jnp.int32, (8, HEAD_DIM), 0)

    def bcast_row(h, r, n):
        return jnp.broadcast_to(b_ref[h, r:r + 1, :], (n, HEAD_DIM))

    def chunk_body(c, carry):
        r0 = pl.multiple_of(c * CHUNK, CHUNK)
        for h in range(nh):
            lanes = slice(h * HEAD_DIM, (h + 1) * HEAD_DIM)
            q = q_ref[pl.ds(r0, CHUNK), lanes]
            f = f_ref[pl.ds(r0, CHUNK), lanes]
            v = v_ref[pl.ds(r0, CHUNK), lanes].astype(BF16)
            k = 1.0 - f

            b = jnp.log(f)
            for s in (1, 2, 4, 8, 16, 32):
                b = b + jnp.where(row >= s, pltpu.roll(b, s, 0), 0.0)
            b_ref[h] = b

            k_bf = k.astype(BF16)
            scores = jnp.where(ids == 0, _nt_dot(q.astype(BF16), k_bf), 0.0)
            scores = jnp.where(ids == 1, _nt_dot((q * f).astype(BF16), k_bf), scores)
            for lid, n in enumerate(reversed(LEVELS), start=2):
                if n == 2:
                    parts = [jnp.where(sub < 4, bcast_row(h, 8 * i + 1, 8), bcast_row(h, 8 * i + 5, 8))
                             for i in range(CHUNK // 8)]
                else:
                    parts = [bcast_row(h, blk * 2 * n + n - 1, 2 * n) for blk in range(CHUNK // (2 * n))]
                b_mid = parts[0] if len(parts) == 1 else jnp.concatenate(parts, axis=0)
                e = jnp.exp(-jnp.abs(b - b_mid))
                p = _nt_dot((q * e).astype(BF16), (k * e).astype(BF16))
                scores = jnp.where(ids == lid, p, scores)

            b_end = bcast_row(h, CHUNK - 1, CHUNK)
            st = st_ref[h]
            o = jnp.dot(scores.astype(BF16), v, preferred_element_type=F32)
            o = o + _nt_dot((q * jnp.exp(b)).astype(BF16), st.astype(BF16))
            k_dec = (k * jnp.exp(b_end - b)).astype(BF16)
            st_ref[h] = st * jnp.exp(b_end[:1]) + _tn_dot(v, k_dec)

            y = o * _rms_scale(o) * gn_ref[:, lanes] * og_ref[pl.ds(r0, CHUNK), lanes]
            o_ref[pl.ds(r0, CHUNK), lanes] = y.astype(o_ref.dtype)
        return carry

    lax.fori_loop(0, nchunk, chunk_body, 0)


def _hgrn(act, gnorm, d):
    t = act.shape[0]
    width = d // 2
    nh = min(4, width // HEAD_DIM)
    bw = nh * HEAD_DIM
    ngroups = width // bw
    tr = _tile(t, 512, CHUNK)

    def seg_spec(seg):
        return pl.BlockSpec((tr, bw), lambda g, r, seg=seg: (r, seg * ngroups + g))

    return pl.pallas_call(
        functools.partial(_hgrn_kernel, nh=nh, nchunk=tr // CHUNK),
        out_shape=jax.ShapeDtypeStruct((t, width), BF16),
        grid=(ngroups, t // tr),
        in_specs=[seg_spec(1), seg_spec(2), seg_spec(3), seg_spec(4),
                  pl.BlockSpec((1, bw), lambda g, r: (0, g)),
                  pl.BlockSpec((CHUNK, CHUNK), lambda g, r: (0, 0))],
        out_specs=pl.BlockSpec((tr, bw), lambda g, r: (r, g)),
        scratch_shapes=[pltpu.VMEM((nh, HEAD_DIM, HEAD_DIM), F32),
                        pltpu.VMEM((nh, CHUNK, HEAD_DIM), F32)],
        compiler_params=_params(("parallel", "arbitrary")),
        name="hgrn2",
    )(act, act, act, act, gnorm.reshape(1, width), jnp.asarray(_level_ids()))


def _merge_kernel(yp_ref, yh_ref, wp_ref, wh_ref, ga_ref, gb_ref, o_ref):
    a = jnp.dot(yp_ref[...], wp_ref[...], preferred_element_type=F32)
    b = jnp.dot(yh_ref[...], wh_ref[...], preferred_element_type=F32)
    o_ref[...] = (ga_ref[...] * a + gb_ref[...] * b).astype(o_ref.dtype)


def _merge(y_pool, y_hgrn, w_up_pool, w_up_hgrn, act, d):
    t, kp = y_pool.shape
    kh = y_hgrn.shape[1]
    tm = _tile(t, 1024, 8)
    tn = _tile(d // 2, 512)
    nb = d // tn
    off_a = (act.shape[1] - 2 * d) // tn
    return pl.pallas_call(
        _merge_kernel,
        out_shape=jax.ShapeDtypeStruct((t, d), BF16),
        grid=(t // tm, nb),
        in_specs=[pl.BlockSpec((tm, kp), lambda i, j: (i, 0)),
                  pl.BlockSpec((tm, kh), lambda i, j: (i, 0)),
                  pl.BlockSpec((kp, tn), lambda i, j: (0, j)),
                  pl.BlockSpec((kh, tn), lambda i, j: (0, j)),
                  pl.BlockSpec((tm, tn), lambda i, j: (i, off_a + j)),
                  pl.BlockSpec((tm, tn), lambda i, j: (i, off_a + nb + j))],
        out_specs=pl.BlockSpec((tm, tn), lambda i, j: (i, j)),
        compiler_params=_params(("parallel", "arbitrary")),
        name="gated_merge",
    )(y_pool, y_hgrn, w_up_pool, w_up_hgrn, act, act)


def _outproj_kernel(m_ref, w_ref, x_ref, g_ref, h_ref, v_ref, *, tn, nj):
    j = pl.program_id(1)
    acc = jnp.dot(m_ref[...], w_ref[...], preferred_element_type=F32)
    for jj in range(nj):
        @pl.when(j == jj)
        def _(jj=jj):
            h_ref[:, jj * tn:(jj + 1) * tn] = x_ref[...] + acc

    @pl.when(j == nj - 1)
    def _():
        h = h_ref[...]
        v_ref[...] = (h * _rms_scale(h) * g_ref[...]).astype(v_ref.dtype)


def _outproj(m, w, x, g):
    t, d = x.shape
    tm = _tile(t, 512, 8)
    tn = _tile(d, 512)
    nj = d // tn
    return pl.pallas_call(
        functools.partial(_outproj_kernel, tn=tn, nj=nj),
        out_shape=(jax.ShapeDtypeStruct((t, d), F32), jax.ShapeDtypeStruct((t, d), BF16)),
        grid=(t // tm, nj),
        in_specs=[pl.BlockSpec((tm, d), lambda i, j: (i, 0)),
                  pl.BlockSpec((d, tn), lambda i, j: (0, j)),
                  pl.BlockSpec((tm, tn), lambda i, j: (i, j)),
                  pl.BlockSpec((1, d), lambda i, j: (0, 0))],
        out_specs=(pl.BlockSpec((tm, d), lambda i, j: (i, 0)),
                   pl.BlockSpec((tm, d), lambda i, j: (i, 0))),
        compiler_params=_params(("parallel", "arbitrary")),
        name="out_proj",
    )(m, w, x, g.reshape(1, d))


def _ffn_kernel(v_ref, wg_ref, wu_ref, wd_ref, h_ref, g_ref, o_ref, *, nj, final_norm):
    j = pl.program_id(1)
    v = v_ref[...]
    gate = jnp.dot(v, wg_ref[...], preferred_element_type=F32)
    up = jnp.dot(v, wu_ref[...], preferred_element_type=F32)
    a = (gate * _sigmoid(gate) * up).astype(BF16)

    @pl.when(j == 0)
    def _():
        o_ref[...] = h_ref[...] + jnp.dot(a, wd_ref[...], preferred_element_type=F32)

    @pl.when(j > 0)
    def _():
        o_ref[...] += jnp.dot(a, wd_ref[...], preferred_element_type=F32)

    if final_norm:
        @pl.when(j == nj - 1)
        def _():
            h = o_ref[...]
            o_ref[...] = h * _rms_scale(h) * g_ref[...]


def _ffn(v, wg, wu, wd, h, g_final, final_norm):
    t, d = h.shape
    ff = wg.shape[1]
    tm = _tile(t, 512, 8)
    tf = _tile(ff, 256)
    nj = ff // tf
    return pl.pallas_call(
        functools.partial(_ffn_kernel, nj=nj, final_norm=final_norm),
        out_shape=jax.ShapeDtypeStruct((t, d), F32),
        grid=(t // tm, nj),
        in_specs=[pl.BlockSpec((tm, d), lambda i, j: (i, 0)),
                  pl.BlockSpec((d, tf), lambda i, j: (0, j)),
                  pl.BlockSpec((d, tf), lambda i, j: (0, j)),
                  pl.BlockSpec((tf, d), lambda i, j: (j, 0)),
                  pl.BlockSpec((tm, d), lambda i, j: (i, 0), pipeline_mode=pl.Buffered(1)),
                  pl.BlockSpec((1, d), lambda i, j: (0, 0))],
        out_specs=pl.BlockSpec((tm, d), lambda i, j: (i, 0)),
        compiler_params=_params(("parallel", "arbitrary")),
        name="swiglu_ffn",
    )(v, wg, wu, wd, h, g_final.reshape(1, d))


def kernel(x, g_mix, w_in, w_pool_group, pool_scale, lb_param, hgrn_norm, w_up_pool, w_up_hgrn, w_out,
           g_ffn, w_ffn_gate, w_ffn_up, w_ffn_down, g_final):
    bsz, t, d = x.shape
    depth = g_mix.shape[0]
    assert d % (2 * HEAD_DIM) == 0 and t % CHUNK == 0
    outs = []
    for bi in range(bsz):
        h = x[bi]
        for l in range(depth):
            last = l == depth - 1
            u = _rmsnorm(h, g_mix[l], BF16)
            act = _proj(u, w_in[l].astype(BF16), lb_param, l)
            y_pool = _pool(act, w_pool_group[l].astype(BF16), pool_scale[l])
            y_hgrn = _hgrn(act, hgrn_norm[l], d)
            merged = _merge(y_pool, y_hgrn, w_up_pool[l].astype(BF16), w_up_hgrn[l].astype(BF16), act, d)
            h1, v = _outproj(merged, w_out[l].astype(BF16), h, g_ffn[l])
            h = _ffn(v, w_ffn_gate[l].astype(BF16), w_ffn_up[l].astype(BF16), w_ffn_down[l].astype(BF16),
                     h1, g_final, last)
        outs.append(h)
    return jnp.stack(outs, axis=0)
```

```python
import functools

import numpy as np
import jax
import jax.numpy as jnp
from jax import lax
from jax.experimental import pallas as pl
from jax.experimental.pallas import tpu as pltpu

EPS = 1e-6
POOL_WINDOWS = (2, 4, 8, 16)
POOL_HALO = 16
HEAD_DIM = 128
CHUNK = 64
LEVELS = (32, 16, 8, 4, 2)

F32 = jnp.float32
BF16 = jnp.bfloat16

VMEM_LIMIT_BYTES = 60 * 1024 * 1024


def _tile(dim, pref, align=128):
    if dim <= pref:
        return dim
    t = (pref // align) * align
    while t >= align:
        if dim % t == 0:
            return t
        t -= align
    return dim


def _params(sem):
    return pltpu.CompilerParams(dimension_semantics=sem, vmem_limit_bytes=VMEM_LIMIT_BYTES)


def _sigmoid(x):
    return 1.0 / (1.0 + jnp.exp(-x))


def _rms_scale(h):
    return lax.rsqrt(jnp.mean(h * h, axis=-1, keepdims=True) + EPS)


def _rmsnorm_kernel(x_ref, g_ref, o_ref):
    x = x_ref[...]
    o_ref[...] = (x * _rms_scale(x) * g_ref[...]).astype(o_ref.dtype)


def _rmsnorm(x, g, out_dtype):
    t, d = x.shape
    tm = _tile(t, 256, 8)
    return pl.pallas_call(
        _rmsnorm_kernel,
        out_shape=jax.ShapeDtypeStruct((t, d), out_dtype),
        grid=(t // tm,),
        in_specs=[pl.BlockSpec((tm, d), lambda i: (i, 0)),
                  pl.BlockSpec((1, d), lambda i: (0, 0))],
        out_specs=pl.BlockSpec((tm, d), lambda i: (i, 0)),
        compiler_params=_params(("parallel",)),
        name="rmsnorm",
    )(x, g.reshape(1, d))


def _proj_kernel(u_ref, w_ref, lbp_ref, o_ref, *, layer, tn, half):
    seg = (pl.program_id(1) * tn) // half
    p = lbp_ref[...]
    e = jnp.exp(p - jnp.max(p, axis=0, keepdims=True))
    lb = jnp.sum(e[: layer + 1], axis=0, keepdims=True) / jnp.sum(e, axis=0, keepdims=True)
    is_f = seg == 2
    is_silu = jnp.logical_or(seg == 1, seg == 4)
    c0 = jnp.where(is_f, lb, 0.0)
    c1 = jnp.where(jnp.logical_or(seg == 0, seg == 3), 1.0, 0.0)
    c2 = jnp.where(is_f, 1.0 - lb, jnp.where(seg >= 5, 1.0, 0.0))
    c3 = jnp.where(is_silu, 1.0, 0.0)
    acc = jnp.dot(u_ref[...], w_ref[0], preferred_element_type=F32)
    o_ref[...] = c0 + c1 * acc + _sigmoid(acc) * (c2 + c3 * acc)


def _proj_tn(d):
    return _tile(d // 2, 1024)


def _proj(u, w, lb_param, layer):
    t, d = u.shape
    half = d // 2
    tm = _tile(t, 1024, 8)
    tn = _proj_tn(d)
    n_in = w.shape[0] * tn
    nb = half // tn

    def lbp_map(i, j):
        return (0, jnp.clip(j - 2 * nb, 0, nb - 1))

    return pl.pallas_call(
        functools.partial(_proj_kernel, layer=layer, tn=tn, half=half),
        out_shape=jax.ShapeDtypeStruct((t, n_in), F32),
        grid=(t // tm, n_in // tn),
        in_specs=[pl.BlockSpec((tm, d), lambda i, j: (i, 0)),
                  pl.BlockSpec((1, d, tn), lambda i, j: (j, 0, 0)),
                  pl.BlockSpec((lb_param.shape[0], tn), lbp_map)],
        out_specs=pl.BlockSpec((tm, tn), lambda i, j: (i, j)),
        compiler_params=_params(("parallel", "arbitrary")),
        name="in_proj",
    )(u, w, lb_param)


def _pool_kernel(z_ref, halo_ref, w_ref, s_ref, o_ref, *, tp):
    i = pl.program_id(0)
    g = pl.program_id(1)
    z = z_ref[...]
    halo = jnp.where(i > 0, halo_ref[...], 0.0)
    full = jnp.concatenate([halo, z], axis=0)
    t = i * tp + lax.broadcasted_iota(jnp.int32, (tp, 1), 0)

    for gi, w in enumerate(POOL_WINDOWS):
        @pl.when(g == gi)
        def _(w=w):
            s = full
            k = 1
            while k < w:
                s = s + pltpu.roll(s, k, 0)
                k *= 2
            count = jnp.minimum(t + 1, w).astype(F32)
            pooled = s[POOL_HALO:] / count - z
            y = jnp.dot(pooled.astype(BF16), w_ref[0], preferred_element_type=F32)
            o_ref[...] = (y * s_ref[0]).astype(o_ref.dtype)


def _pool(act, w_group, scale):
    t = act.shape[0]
    ng, gsz, _ = w_group.shape
    assert ng == len(POOL_WINDOWS)
    tp = _tile(t, 512, POOL_HALO)
    hb = tp // POOL_HALO
    return pl.pallas_call(
        functools.partial(_pool_kernel, tp=tp),
        out_shape=jax.ShapeDtypeStruct((t, ng * gsz), BF16),
        grid=(t // tp, ng),
        in_specs=[pl.BlockSpec((tp, gsz), lambda i, g: (i, g)),
                  pl.BlockSpec((POOL_HALO, gsz), lambda i, g: (jnp.maximum(i * hb - 1, 0), g)),
                  pl.BlockSpec((1, gsz, gsz), lambda i, g: (g, 0, 0)),
                  pl.BlockSpec((1, 1, gsz), lambda i, g: (g, 0, 0))],
        out_specs=pl.BlockSpec((tp, gsz), lambda i, g: (i, g)),
        compiler_params=_params(("parallel", "arbitrary")),
        name="pool_mixer",
    )(act, act, w_group, scale.reshape(ng, 1, gsz))


def _level_ids():
    t = np.arange(CHUNK)[:, None]
    s = np.arange(CHUNK)[None, :]
    ids = -np.ones((CHUNK, CHUNK), np.int32)
    ids[t == s] = 0
    for lid, n in enumerate((1,) + tuple(reversed(LEVELS)), start=1):
        m = (t // (2 * n) == s // (2 * n)) & (t % (2 * n) >= n) & (s % (2 * n) < n)
        ids[m] = lid
    return ids


def _nt_dot(a, b):
    return lax.dot_general(a, b, (((1,), (1,)), ((), ())), preferred_element_type=F32)


def _tn_dot(a, b):
    return lax.dot_general(a, b, (((0,), (0,)), ((), ())), preferred_element_type=F32)


def _hgrn_kernel(q_ref, f_ref, v_ref, og_ref, gn_ref, ids_ref, o_ref, st_ref, b_ref, *, nh, nchunk):
    @pl.when(pl.program_id(1) == 0)
    def _():
        st_ref[...] = jnp.zeros_like(st_ref)

    ids = ids_ref[...]
    row = lax.broadcasted_iota(jnp.int32, (CHUNK, HEAD_DIM), 0)
    sub = lax.broadcasted_iota(jnp.int32, (8, HEAD_DIM), 0)

    def bcast_row(h, r, n):
        return jnp.broadcast_to(b_ref[h, r:r + 1, :], (n, HEAD_DIM))

    def chunk_body(c, carry):
        r0 = pl.multiple_of(c * CHUNK, CHUNK)
        for h in range(nh):
            lanes = slice(h * HEAD_DIM, (h + 1) * HEAD_DIM)
            q = q_ref[pl.ds(r0, CHUNK), lanes]
            f = f_ref[pl.ds(r0, CHUNK), lanes]
            v = v_ref[pl.ds(r0, CHUNK), lanes].astype(BF16)
            k = 1.0 - f

            b = jnp.log(f)
            for s in (1, 2, 4, 8, 16, 32):
                b = b + jnp.where(row >= s, pltpu.roll(b, s, 0), 0.0)
            b_ref[h] = b

            k_bf = k.astype(BF16)
            scores = jnp.where(ids == 0, _nt_dot(q.astype(BF16), k_bf), 0.0)
            scores = jnp.where(ids == 1, _nt_dot((q * f).astype(BF16), k_bf), scores)
            for lid, n in enumerate(reversed(LEVELS), start=2):
                if n == 2:
                    parts = [jnp.where(sub < 4, bcast_row(h, 8 * i + 1, 8), bcast_row(h, 8 * i + 5, 8))
                             for i in range(CHUNK // 8)]
                else:
                    parts = [bcast_row(h, blk * 2 * n + n - 1, 2 * n) for blk in range(CHUNK // (2 * n))]
                b_mid = parts[0] if len(parts) == 1 else jnp.concatenate(parts, axis=0)
                e = jnp.exp(-jnp.abs(b - b_mid))
                p = _nt_dot((q * e).astype(BF16), (k * e).astype(BF16))
                scores = jnp.where(ids == lid, p, scores)

            b_end = bcast_row(h, CHUNK - 1, CHUNK)
            st = st_ref[h]
            o = jnp.dot(scores.astype(BF16), v, preferred_element_type=F32)
            o = o + _nt_dot((q * jnp.exp(b)).astype(BF16), st.astype(BF16))
            k_dec = (k * jnp.exp(b_end - b)).astype(BF16)
            st_ref[h] = st * jnp.exp(b_end[:1]) + _tn_dot(v, k_dec)

            y = o * _rms_scale(o) * gn_ref[:, lanes] * og_ref[pl.ds(r0, CHUNK), lanes]
            o_ref[pl.ds(r0, CHUNK), lanes] = y.astype(o_ref.dtype)
        return carry

    lax.fori_loop(0, nchunk, chunk_body, 0)


def _hgrn(act, gnorm, d):
    t = act.shape[0]
    width = d // 2
    nh = min(4, width // HEAD_DIM)
    bw = nh * HEAD_DIM
    ngroups = width // bw
    tr = _tile(t, 512, CHUNK)

    def seg_spec(seg):
        return pl.BlockSpec((tr, bw), lambda g, r, seg=seg: (r, seg * ngroups + g))

    return pl.pallas_call(
        functools.partial(_hgrn_kernel, nh=nh, nchunk=tr // CHUNK),
        out_shape=jax.ShapeDtypeStruct((t, width), BF16),
        grid=(ngroups, t // tr),
        in_specs=[seg_spec(1), seg_spec(2), seg_spec(3), seg_spec(4),
                  pl.BlockSpec((1, bw), lambda g, r: (0, g)),
                  pl.BlockSpec((CHUNK, CHUNK), lambda g, r: (0, 0))],
        out_specs=pl.BlockSpec((tr, bw), lambda g, r: (r, g)),
        scratch_shapes=[pltpu.VMEM((nh, HEAD_DIM, HEAD_DIM), F32),
                        pltpu.VMEM((nh, CHUNK, HEAD_DIM), F32)],
        compiler_params=_params(("parallel", "arbitrary")),
        name="hgrn2",
    )(act, act, act, act, gnorm.reshape(1, width), jnp.asarray(_level_ids()))


def _merge_kernel(yp_ref, yh_ref, wp_ref, wh_ref, ga_ref, gb_ref, o_ref):
    a = jnp.dot(yp_ref[...], wp_ref[...], preferred_element_type=F32)
    b = jnp.dot(yh_ref[...], wh_ref[...], preferred_element_type=F32)
    o_ref[...] = (ga_ref[...] * a + gb_ref[...] * b).astype(o_ref.dtype)


def _merge(y_pool, y_hgrn, w_up_pool, w_up_hgrn, act, d):
    t, kp = y_pool.shape
    kh = y_hgrn.shape[1]
    tm = _tile(t, 1024, 8)
    tn = _tile(d // 2, 512)
    nb = d // tn
    off_a = (act.shape[1] - 2 * d) // tn
    return pl.pallas_call(
        _merge_kernel,
        out_shape=jax.ShapeDtypeStruct((t, d), BF16),
        grid=(t // tm, nb),
        in_specs=[pl.BlockSpec((tm, kp), lambda i, j: (i, 0)),
                  pl.BlockSpec((tm, kh), lambda i, j: (i, 0)),
                  pl.BlockSpec((kp, tn), lambda i, j: (0, j)),
                  pl.BlockSpec((kh, tn), lambda i, j: (0, j)),
                  pl.BlockSpec((tm, tn), lambda i, j: (i, off_a + j)),
                  pl.BlockSpec((tm, tn), lambda i, j: (i, off_a + nb + j))],
        out_specs=pl.BlockSpec((tm, tn), lambda i, j: (i, j)),
        compiler_params=_params(("parallel", "arbitrary")),
        name="gated_merge",
    )(y_pool, y_hgrn, w_up_pool, w_up_hgrn, act, act)


def _outproj_kernel(m_ref, w_ref, x_ref, g_ref, h_ref, v_ref, *, tn, nj):
    j = pl.program_id(1)
    acc = jnp.dot(m_ref[...], w_ref[...], preferred_element_type=F32)
    for jj in range(nj):
        @pl.when(j == jj)
        def _(jj=jj):
            h_ref[:, jj * tn:(jj + 1) * tn] = x_ref[...] + acc

    @pl.when(j == nj - 1)
    def _():
        h = h_ref[...]
        v_ref[...] = (h * _rms_scale(h) * g_ref[...]).astype(v_ref.dtype)


def _outproj(m, w, x, g):
    t, d = x.shape
    tm = _tile(t, 512, 8)
    tn = _tile(d, 512)
    nj = d // tn
    return pl.pallas_call(
        functools.partial(_outproj_kernel, tn=tn, nj=nj),
        out_shape=(jax.ShapeDtypeStruct((t, d), F32), jax.ShapeDtypeStruct((t, d), BF16)),
        grid=(t // tm, nj),
        in_specs=[pl.BlockSpec((tm, d), lambda i, j: (i, 0)),
                  pl.BlockSpec((d, tn), lambda i, j: (0, j)),
                  pl.BlockSpec((tm, tn), lambda i, j: (i, j)),
                  pl.BlockSpec((1, d), lambda i, j: (0, 0))],
        out_specs=(pl.BlockSpec((tm, d), lambda i, j: (i, 0)),
                   pl.BlockSpec((tm, d), lambda i, j: (i, 0))),
        compiler_params=_params(("parallel", "arbitrary")),
        name="out_proj",
    )(m, w, x, g.reshape(1, d))


def _ffn_kernel(v_ref, wgu_ref, wd_ref, h_ref, g_ref, o_ref, *, nj, tf, final_norm):
    j = pl.program_id(1)

    @pl.when(j == 0)
    def _():
        o_ref[...] = h_ref[...]

    gu = jnp.dot(v_ref[...], wgu_ref[0], preferred_element_type=F32)
    gate = gu[:, :tf]
    a = (gate * _sigmoid(gate) * gu[:, tf:]).astype(BF16)
    o_ref[...] += jnp.dot(a, wd_ref[...], preferred_element_type=F32)

    if final_norm:
        @pl.when(j == nj - 1)
        def _():
            h = o_ref[...]
            o_ref[...] = h * _rms_scale(h) * g_ref[...]


def _ffn_tf(ff):
    return _tile(ff, 256)


def _ffn(v, wgu, wd, h, g_final, final_norm):
    t, d = h.shape
    ff = wd.shape[0]
    tm = _tile(t, 512, 8)
    tf = _ffn_tf(ff)
    nj = ff // tf
    return pl.pallas_call(
        functools.partial(_ffn_kernel, nj=nj, tf=tf, final_norm=final_norm),
        out_shape=jax.ShapeDtypeStruct((t, d), F32),
        grid=(t // tm, nj),
        in_specs=[pl.BlockSpec((tm, d), lambda i, j: (i, 0)),
                  pl.BlockSpec((1, d, 2 * tf), lambda i, j: (j, 0, 0)),
                  pl.BlockSpec((tf, d), lambda i, j: (j, 0)),
                  pl.BlockSpec((tm, d), lambda i, j: (i, 0), pipeline_mode=pl.Buffered(1)),
                  pl.BlockSpec((1, d), lambda i, j: (0, 0))],
        out_specs=pl.BlockSpec((tm, d), lambda i, j: (i, 0)),
        compiler_params=_params(("parallel", "arbitrary")),
        name="swiglu_ffn",
    )(v, wgu, wd, h, g_final.reshape(1, d))


def _tile_major(w, tn):
    k, n = w.shape
    return w.astype(BF16).reshape(k, n // tn, tn).transpose(1, 0, 2)


def kernel(x, g_mix, w_in, w_pool_group, pool_scale, lb_param, hgrn_norm, w_up_pool, w_up_hgrn, w_out,
           g_ffn, w_ffn_gate, w_ffn_up, w_ffn_down, g_final):
    bsz, t, d = x.shape
    depth = g_mix.shape[0]
    assert d % (2 * HEAD_DIM) == 0 and t % CHUNK == 0
    tf = _ffn_tf(w_ffn_gate.shape[2])
    outs = []
    for bi in range(bsz):
        h = x[bi]
        for l in range(depth):
            last = l == depth - 1
            wgu = jnp.concatenate([_tile_major(w_ffn_gate[l], tf), _tile_major(w_ffn_up[l], tf)], axis=2)
            u = _rmsnorm(h, g_mix[l], BF16)
            act = _proj(u, _tile_major(w_in[l], _proj_tn(d)), lb_param, l)
            y_pool = _pool(act, w_pool_group[l].astype(BF16), pool_scale[l])
            y_hgrn = _hgrn(act, hgrn_norm[l], d)
            merged = _merge(y_pool, y_hgrn, w_up_pool[l].astype(BF16), w_up_hgrn[l].astype(BF16), act, d)
            h1, v = _outproj(merged, w_out[l].astype(BF16), h, g_ffn[l])
            h = _ffn(v, wgu, w_ffn_down[l].astype(BF16), h1, g_final, last)
        outs.append(h)
    return jnp.stack(outs, axis=0)
```

```python
import functools

import numpy as np
import jax
import jax.numpy as jnp
from jax import lax
from jax.experimental import pallas as pl
from jax.experimental.pallas import tpu as pltpu

EPS = 1e-6
POOL_WINDOWS = (2, 4, 8, 16)
POOL_HALO = 16
HEAD_DIM = 128
CHUNK = 64
NORM_ROWS = 256
LEVELS = (32, 16, 8, 4, 2)

F32 = jnp.float32
BF16 = jnp.bfloat16

VMEM_LIMIT_BYTES = 60 * 1024 * 1024


def _tile(dim, pref, align=128):
    if dim <= pref:
        return dim
    t = (pref // align) * align
    while t >= align:
        if dim % t == 0:
            return t
        t -= align
    return dim


def _params(sem):
    return pltpu.CompilerParams(dimension_semantics=sem, vmem_limit_bytes=VMEM_LIMIT_BYTES)


def _sigmoid(x):
    return 1.0 / (1.0 + jnp.exp(-x))


def _rms_scale(h):
    return lax.rsqrt(jnp.mean(h * h, axis=-1, keepdims=True) + EPS)


def _rmsnorm_kernel(x_ref, g_ref, o_ref):
    x = x_ref[...]
    o_ref[...] = (x * _rms_scale(x) * g_ref[...]).astype(o_ref.dtype)


def _rmsnorm(x, g, out_dtype):
    t, d = x.shape
    tm = _tile(t, 256, 8)
    return pl.pallas_call(
        _rmsnorm_kernel,
        out_shape=jax.ShapeDtypeStruct((t, d), out_dtype),
        grid=(t // tm,),
        in_specs=[pl.BlockSpec((tm, d), lambda i: (i, 0)),
                  pl.BlockSpec((1, d), lambda i: (0, 0))],
        out_specs=pl.BlockSpec((tm, d), lambda i: (i, 0)),
        compiler_params=_params(("parallel",)),
        name="rmsnorm",
    )(x, g.reshape(1, d))


def _proj_kernel(u_ref, w_ref, lbp_ref, o_ref, acc_ref, *, layer, tn, half, nj):
    s = pl.program_id(0)

    @pl.when(s == 0)
    def _():
        acc_ref[...] = jnp.zeros_like(acc_ref)

    seg = ((jnp.maximum(s - 1, 0) % nj) * tn) // half
    p = lbp_ref[...]
    e = jnp.exp(p - jnp.max(p, axis=0, keepdims=True))
    lb = jnp.sum(e[: layer + 1], axis=0, keepdims=True) / jnp.sum(e, axis=0, keepdims=True)
    is_f = seg == 2
    is_silu = jnp.logical_or(seg == 1, seg == 4)
    c0 = jnp.where(is_f, lb, 0.0)
    c1 = jnp.where(jnp.logical_or(seg == 0, seg == 3), 1.0, 0.0)
    c2 = jnp.where(is_f, 1.0 - lb, jnp.where(seg >= 5, 1.0, 0.0))
    c3 = jnp.where(is_silu, 1.0, 0.0)
    acc = acc_ref[...]
    o_ref[...] = c0 + c1 * acc + _sigmoid(acc) * (c2 + c3 * acc)
    acc_ref[...] = jnp.dot(u_ref[...], w_ref[...].astype(BF16), preferred_element_type=F32)


def _proj(u, w, lb_param, layer):
    t, d = u.shape
    n_in = w.shape[1]
    half = d // 2
    tm = _tile(t, 1024, 8)
    tn = _tile(half, 512)
    nb = half // tn
    nj = n_in // tn
    nsteps = (t // tm) * nj

    def cur(s):
        return jnp.minimum(s, nsteps - 1)

    def prev(s):
        return jnp.maximum(s - 1, 0)

    return pl.pallas_call(
        functools.partial(_proj_kernel, layer=layer, tn=tn, half=half, nj=nj),
        out_shape=jax.ShapeDtypeStruct((t, n_in), F32),
        grid=(nsteps + 1,),
        in_specs=[pl.BlockSpec((tm, d), lambda s: (cur(s) // nj, 0)),
                  pl.BlockSpec((d, tn), lambda s: (0, cur(s) % nj)),
                  pl.BlockSpec((lb_param.shape[0], tn),
                               lambda s: (0, jnp.clip(prev(s) % nj - 2 * nb, 0, nb - 1)))],
        out_specs=pl.BlockSpec((tm, tn), lambda s: (prev(s) // nj, prev(s) % nj)),
        scratch_shapes=[pltpu.VMEM((tm, tn), F32)],
        compiler_params=_params(("arbitrary",)),
        name="in_proj",
    )(u, w, lb_param)


def _pool_kernel(z_ref, halo_ref, w_ref, s_ref, o_ref, *, tp):
    i = pl.program_id(0)
    g = pl.program_id(1)
    z = z_ref[...]
    halo = jnp.where(i > 0, halo_ref[...], 0.0)
    full = jnp.concatenate([halo, z], axis=0)
    t = i * tp + lax.broadcasted_iota(jnp.int32, (tp, 1), 0)

    for gi, w in enumerate(POOL_WINDOWS):
        @pl.when(g == gi)
        def _(w=w):
            s = full
            k = 1
            while k < w:
                s = s + pltpu.roll(s, k, 0)
                k *= 2
            count = jnp.minimum(t + 1, w).astype(F32)
            pooled = s[POOL_HALO:] / count - z
            y = jnp.dot(pooled.astype(BF16), w_ref[0].astype(BF16), preferred_element_type=F32)
            o_ref[...] = (y * s_ref[0]).astype(o_ref.dtype)


def _pool(act, w_group, scale):
    t = act.shape[0]
    ng, gsz, _ = w_group.shape
    assert ng == len(POOL_WINDOWS)
    tp = _tile(t, 512, POOL_HALO)
    hb = tp // POOL_HALO
    return pl.pallas_call(
        functools.partial(_pool_kernel, tp=tp),
        out_shape=jax.ShapeDtypeStruct((t, ng * gsz), BF16),
        grid=(t // tp, ng),
        in_specs=[pl.BlockSpec((tp, gsz), lambda i, g: (i, g)),
                  pl.BlockSpec((POOL_HALO, gsz), lambda i, g: (jnp.maximum(i * hb - 1, 0), g)),
                  pl.BlockSpec((1, gsz, gsz), lambda i, g: (g, 0, 0)),
                  pl.BlockSpec((1, 1, gsz), lambda i, g: (g, 0, 0))],
        out_specs=pl.BlockSpec((tp, gsz), lambda i, g: (i, g)),
        compiler_params=_params(("parallel", "arbitrary")),
        name="pool_mixer",
    )(act, act, w_group, scale.reshape(ng, 1, gsz))


def _level_ids():
    t = np.arange(CHUNK)[:, None]
    s = np.arange(CHUNK)[None, :]
    ids = -np.ones((CHUNK, CHUNK), np.int32)
    ids[t == s] = 0
    for lid, n in enumerate((1,) + tuple(reversed(LEVELS)), start=1):
        m = (t // (2 * n) == s // (2 * n)) & (t % (2 * n) >= n) & (s % (2 * n) < n)
        ids[m] = lid
    return ids


def _nt_dot(a, b):
    return lax.dot_general(a, b, (((1,), (1,)), ((), ())), preferred_element_type=F32)


def _tn_dot(a, b):
    return lax.dot_general(a, b, (((0,), (0,)), ((), ())), preferred_element_type=F32)


def _hgrn_kernel(q_ref, f_ref, v_ref, og_ref, gn_ref, ids_ref, o_ref, st_ref, b_ref, *, nh, nchunk):
    @pl.when(pl.program_id(1) == 0)
    def _():
        st_ref[...] = jnp.zeros_like(st_ref)

    ids = ids_ref[...]
    row = lax.broadcasted_iota(jnp.int32, (CHUNK, HEAD_DIM), 0)
    sub = lax.broadcasted_iota(jnp.int32, (8, HEAD_DIM), 0)

    def bcast_row(h, r, n):
        return jnp.broadcast_to(b_ref[h, r:r + 1, :], (n, HEAD_DIM))

    def chunk_body(c, carry):
        r0 = pl.multiple_of(c * CHUNK, CHUNK)
        for h in range(nh):
            lanes = slice(h * HEAD_DIM, (h + 1) * HEAD_DIM)
            q = q_ref[pl.ds(r0, CHUNK), lanes]
            f = f_ref[pl.ds(r0, CHUNK), lanes]
            v = v_ref[pl.ds(r0, CHUNK), lanes].astype(BF16)
            k = 1.0 - f

            b = jnp.log(f)
            for s in (1, 2, 4, 8, 16, 32):
                b = b + jnp.where(row >= s, pltpu.roll(b, s, 0), 0.0)
            b_ref[h] = b

            k_bf = k.astype(BF16)
            scores = jnp.where(ids == 0, _nt_dot(q.astype(BF16), k_bf), 0.0)
            scores = jnp.where(ids == 1, _nt_dot((q * f).astype(BF16), k_bf), scores)
            for lid, n in enumerate(reversed(LEVELS), start=2):
                if n == 2:
                    parts = [jnp.where(sub < 4, bcast_row(h, 8 * i + 1, 8), bcast_row(h, 8 * i + 5, 8))
                             for i in range(CHUNK // 8)]
                else:
                    parts = [bcast_row(h, blk * 2 * n + n - 1, 2 * n) for blk in range(CHUNK // (2 * n))]
                b_mid = parts[0] if len(parts) == 1 else jnp.concatenate(parts, axis=0)
                e = jnp.exp(-jnp.abs(b - b_mid))
                p = _nt_dot((q * e).astype(BF16), (k * e).astype(BF16))
                scores = jnp.where(ids == lid, p, scores)

            b_end = bcast_row(h, CHUNK - 1, CHUNK)
            st = st_ref[h]
            o = jnp.dot(scores.astype(BF16), v, preferred_element_type=F32)
            o = o + _nt_dot((q * jnp.exp(b)).astype(BF16), st.astype(BF16))
            k_dec = (k * jnp.exp(b_end - b)).astype(BF16)
            st_ref[h] = st * jnp.exp(b_end[:1]) + _tn_dot(v, k_dec)

            y = o * _rms_scale(o) * gn_ref[:, lanes] * og_ref[pl.ds(r0, CHUNK), lanes]
            o_ref[pl.ds(r0, CHUNK), lanes] = y.astype(o_ref.dtype)
        return carry

    lax.fori_loop(0, nchunk, chunk_body, 0)


def _hgrn(act, gnorm, d):
    t = act.shape[0]
    width = d // 2
    nh = min(4, width // HEAD_DIM)
    bw = nh * HEAD_DIM
    ngroups = width // bw
    tr = _tile(t, 512, CHUNK)

    def seg_spec(seg):
        return pl.BlockSpec((tr, bw), lambda g, r, seg=seg: (r, seg * ngroups + g))

    return pl.pallas_call(
        functools.partial(_hgrn_kernel, nh=nh, nchunk=tr // CHUNK),
        out_shape=jax.ShapeDtypeStruct((t, width), BF16),
        grid=(ngroups, t // tr),
        in_specs=[seg_spec(1), seg_spec(2), seg_spec(3), seg_spec(4),
                  pl.BlockSpec((1, bw), lambda g, r: (0, g)),
                  pl.BlockSpec((CHUNK, CHUNK), lambda g, r: (0, 0))],
        out_specs=pl.BlockSpec((tr, bw), lambda g, r: (r, g)),
        scratch_shapes=[pltpu.VMEM((nh, HEAD_DIM, HEAD_DIM), F32),
                        pltpu.VMEM((nh, CHUNK, HEAD_DIM), F32)],
        compiler_params=_params(("parallel", "arbitrary")),
        name="hgrn2",
    )(act, act, act, act, gnorm.reshape(1, width), jnp.asarray(_level_ids()))


def _merge_kernel(yp_ref, yh_ref, wp_ref, wh_ref, ga_ref, gb_ref, o_ref):
    a = jnp.dot(yp_ref[...], wp_ref[...].astype(BF16), preferred_element_type=F32)
    b = jnp.dot(yh_ref[...], wh_ref[...].astype(BF16), preferred_element_type=F32)
    o_ref[...] = (ga_ref[...] * a + gb_ref[...] * b).astype(o_ref.dtype)


def _merge(y_pool, y_hgrn, w_up_pool, w_up_hgrn, act, d):
    t, kp = y_pool.shape
    kh = y_hgrn.shape[1]
    tm = _tile(t, 1024, 8)
    tn = _tile(d // 2, 512)
    nb = d // tn
    off_a = (act.shape[1] - 2 * d) // tn
    return pl.pallas_call(
        _merge_kernel,
        out_shape=jax.ShapeDtypeStruct((t, d), BF16),
        grid=(t // tm, nb),
        in_specs=[pl.BlockSpec((tm, kp), lambda i, j: (i, 0)),
                  pl.BlockSpec((tm, kh), lambda i, j: (i, 0)),
                  pl.BlockSpec((kp, tn), lambda i, j: (0, j)),
                  pl.BlockSpec((kh, tn), lambda i, j: (0, j)),
                  pl.BlockSpec((tm, tn), lambda i, j: (i, off_a + j)),
                  pl.BlockSpec((tm, tn), lambda i, j: (i, off_a + nb + j))],
        out_specs=pl.BlockSpec((tm, tn), lambda i, j: (i, j)),
        compiler_params=_params(("parallel", "arbitrary")),
        name="gated_merge",
    )(y_pool, y_hgrn, w_up_pool, w_up_hgrn, act, act)


def _outproj_kernel(m_ref, w_ref, x_ref, h_ref):
    h_ref[...] = x_ref[...] + jnp.dot(m_ref[...], w_ref[...].astype(BF16), preferred_element_type=F32)


def _outproj(m, w, x):
    t, d = x.shape
    tm = _tile(t, 1024, 8)
    tn = _tile(d, 512)
    return pl.pallas_call(
        _outproj_kernel,
        out_shape=jax.ShapeDtypeStruct((t, d), F32),
        grid=(t // tm, d // tn),
        in_specs=[pl.BlockSpec((tm, d), lambda i, j: (i, 0)),
                  pl.BlockSpec((d, tn), lambda i, j: (0, j)),
                  pl.BlockSpec((tm, tn), lambda i, j: (i, j))],
        out_specs=pl.BlockSpec((tm, tn), lambda i, j: (i, j)),
        compiler_params=_params(("parallel", "arbitrary")),
        name="out_proj",
    )(m, w, x)


def _ffn_kernel(h_hbm, wg_ref, wu_ref, wd_ref, gin_ref, gout_ref, o_hbm, acc_ref, v_ref, sem,
                *, nj, tm, final_norm):
    i = pl.program_id(0)
    j = pl.program_id(1)
    rows = pl.ds(pl.multiple_of(i * tm, tm), tm)
    chunk = min(tm, NORM_ROWS)

    @pl.when(j == 0)
    def _():
        load = pltpu.make_async_copy(h_hbm.at[rows, :], acc_ref, sem)
        load.start()
        load.wait()
        for r in range(0, tm, chunk):
            h = acc_ref[r:r + chunk, :]
            v_ref[r:r + chunk, :] = (h * _rms_scale(h) * gin_ref[...]).astype(BF16)

    v = v_ref[...]
    gate = jnp.dot(v, wg_ref[...].astype(BF16), preferred_element_type=F32)
    up = jnp.dot(v, wu_ref[...].astype(BF16), preferred_element_type=F32)
    a = (gate * _sigmoid(gate) * up).astype(BF16)
    acc_ref[...] += jnp.dot(a, wd_ref[...].astype(BF16), preferred_element_type=F32)

    @pl.when(j == nj - 1)
    def _():
        if final_norm:
            for r in range(0, tm, chunk):
                h = acc_ref[r:r + chunk, :]
                acc_ref[r:r + chunk, :] = h * _rms_scale(h) * gout_ref[...]
        store = pltpu.make_async_copy(acc_ref, o_hbm.at[rows, :], sem)
        store.start()
        store.wait()


def _ffn(h, wg, wu, wd, g_in, g_out, final_norm):
    t, d = h.shape
    ff = wd.shape[0]
    tm = _tile(t, 1024, 8)
    tf = _tile(ff, 256)
    nj = ff // tf
    return pl.pallas_call(
        functools.partial(_ffn_kernel, nj=nj, tm=tm, final_norm=final_norm),
        out_shape=jax.ShapeDtypeStruct((t, d), F32),
        grid=(t // tm, nj),
        in_specs=[pl.BlockSpec(memory_space=pl.ANY),
                  pl.BlockSpec((d, tf), lambda i, j: (0, j)),
                  pl.BlockSpec((d, tf), lambda i, j: (0, j)),
                  pl.BlockSpec((tf, d), lambda i, j: (j, 0)),
                  pl.BlockSpec((1, d), lambda i, j: (0, 0)),
                  pl.BlockSpec((1, d), lambda i, j: (0, 0))],
        out_specs=pl.BlockSpec(memory_space=pl.ANY),
        scratch_shapes=[pltpu.VMEM((tm, d), F32), pltpu.VMEM((tm, d), BF16), pltpu.SemaphoreType.DMA],
        compiler_params=_params(("arbitrary", "arbitrary")),
        name="swiglu_ffn",
    )(h, wg, wu, wd, g_in.reshape(1, d), g_out.reshape(1, d))


def kernel(x, g_mix, w_in, w_pool_group, pool_scale, lb_param, hgrn_norm, w_up_pool, w_up_hgrn, w_out,
           g_ffn, w_ffn_gate, w_ffn_up, w_ffn_down, g_final):
    bsz, t, d = x.shape
    depth = g_mix.shape[0]
    assert d % (2 * HEAD_DIM) == 0 and t % CHUNK == 0
    outs = []
    for bi in range(bsz):
        h = x[bi]
        for l in range(depth):
            u = _rmsnorm(h, g_mix[l], BF16)
            act = _proj(u, w_in[l], lb_param, l)
            y_pool = _pool(act, w_pool_group[l], pool_scale[l])
            y_hgrn = _hgrn(act, hgrn_norm[l], d)
            merged = _merge(y_pool, y_hgrn, w_up_pool[l], w_up_hgrn[l], act, d)
            h1 = _outproj(merged, w_out[l], h)
            h = _ffn(h1, w_ffn_gate[l], w_ffn_up[l], w_ffn_down[l], g_ffn[l], g_final, l == depth - 1)
        outs.append(h)
    return jnp.stack(outs, axis=0)
```

```python
import functools

import numpy as np
import jax
import jax.numpy as jnp
from jax import lax
from jax.experimental import pallas as pl
from jax.experimental.pallas import tpu as pltpu

EPS = 1e-6
POOL_WINDOWS = (2, 4, 8, 16)
POOL_HALO = 16
HEAD_DIM = 128
CHUNK = 64
NORM_ROWS = 256
PROJ_ROW_SPLITS = 4
HGRN_HEADS_PER_STEP = 16
LEVELS = (32, 16, 8, 4, 2)

LOG2_E = 1.4426950408889634

F32 = jnp.float32
BF16 = jnp.bfloat16

VMEM_LIMIT_BYTES = 60 * 1024 * 1024


def _tile(dim, pref, align=128):
    if dim <= pref:
        return dim
    t = (pref // align) * align
    while t >= align:
        if dim % t == 0:
            return t
        t -= align
    return dim


def _params(sem, flags=None):
    return pltpu.CompilerParams(dimension_semantics=sem, vmem_limit_bytes=VMEM_LIMIT_BYTES, flags=flags)


def _sigmoid(x):
    return 1.0 / (1.0 + jnp.exp(-x))


def _rms_scale(h):
    return lax.rsqrt(jnp.mean(h * h, axis=-1, keepdims=True) + EPS)


def _rmsnorm_kernel(x_ref, g_ref, o_ref):
    x = x_ref[...]
    o_ref[...] = (x * _rms_scale(x) * g_ref[...]).astype(o_ref.dtype)


def _rmsnorm(x, g, out_dtype):
    t, d = x.shape
    tm = _tile(t, 256, 8)
    return pl.pallas_call(
        _rmsnorm_kernel,
        out_shape=jax.ShapeDtypeStruct((t, d), out_dtype),
        grid=(t // tm,),
        in_specs=[pl.BlockSpec((tm, d), lambda i: (i, 0)),
                  pl.BlockSpec((1, d), lambda i: (0, 0))],
        out_specs=pl.BlockSpec((tm, d), lambda i: (i, 0)),
        compiler_params=_params(("parallel",)),
        name="rmsnorm",
    )(x, g.reshape(1, d))


def _proj_kernel(u_ref, w_ref, lbp_ref, o_ref, *, layer, tn, half):
    seg = (pl.program_id(1) * tn) // half
    p = lbp_ref[...]
    e = jnp.exp(p - jnp.max(p, axis=0, keepdims=True))
    lb = jnp.sum(e[: layer + 1], axis=0, keepdims=True) / jnp.sum(e, axis=0, keepdims=True)
    is_f = seg == 2
    is_silu = jnp.logical_or(seg == 1, seg == 4)
    c0 = jnp.where(is_f, lb, 0.0)
    c1 = jnp.where(jnp.logical_or(seg == 0, seg == 3), 1.0, 0.0)
    c2 = jnp.where(is_f, 1.0 - lb, jnp.where(seg >= 5, 1.0, 0.0))
    c3 = jnp.where(is_silu, 1.0, 0.0)
    w = w_ref[...].astype(BF16)
    tm = u_ref.shape[0]
    rows = tm // PROJ_ROW_SPLITS
    for r in range(0, tm, rows):
        acc = jnp.dot(u_ref[r:r + rows, :], w, preferred_element_type=F32)
        o_ref[r:r + rows, :] = c0 + c1 * acc + _sigmoid(acc) * (c2 + c3 * acc)


def _proj(u, w, lb_param, layer):
    t, d = u.shape
    n_in = w.shape[1]
    half = d // 2
    tm = _tile(t, 1024, 8)
    tn = _tile(half, 512)
    nb = half // tn
    return pl.pallas_call(
        functools.partial(_proj_kernel, layer=layer, tn=tn, half=half),
        out_shape=jax.ShapeDtypeStruct((t, n_in), F32),
        grid=(t // tm, n_in // tn),
        in_specs=[pl.BlockSpec((tm, d), lambda i, j: (i, 0)),
                  pl.BlockSpec((d, tn), lambda i, j: (0, j)),
                  pl.BlockSpec((lb_param.shape[0], tn), lambda i, j: (0, jnp.clip(j - 2 * nb, 0, nb - 1)))],
        out_specs=pl.BlockSpec((tm, tn), lambda i, j: (i, j)),
        compiler_params=_params(("parallel", "arbitrary")),
        name="in_proj",
    )(u, w, lb_param)


def _pool_kernel(z_ref, halo_ref, w_ref, s_ref, o_ref, *, tp):
    i = pl.program_id(0)
    g = pl.program_id(1)
    z = z_ref[...]
    halo = jnp.where(i > 0, halo_ref[...], 0.0)
    full = jnp.concatenate([halo, z], axis=0)
    t = i * tp + lax.broadcasted_iota(jnp.int32, (tp, 1), 0)

    for gi, w in enumerate(POOL_WINDOWS):
        @pl.when(g == gi)
        def _(w=w):
            s = full
            k = 1
            while k < w:
                s = s + pltpu.roll(s, k, 0)
                k *= 2
            count = jnp.minimum(t + 1, w).astype(F32)
            pooled = s[POOL_HALO:] / count - z
            y = jnp.dot(pooled.astype(BF16), w_ref[0].astype(BF16), preferred_element_type=F32)
            o_ref[...] = (y * s_ref[0]).astype(o_ref.dtype)


def _pool(act, w_group, scale):
    t = act.shape[0]
    ng, gsz, _ = w_group.shape
    assert ng == len(POOL_WINDOWS)
    tp = _tile(t, 512, POOL_HALO)
    hb = tp // POOL_HALO
    return pl.pallas_call(
        functools.partial(_pool_kernel, tp=tp),
        out_shape=jax.ShapeDtypeStruct((t, ng * gsz), BF16),
        grid=(t // tp, ng),
        in_specs=[pl.BlockSpec((tp, gsz), lambda i, g: (i, g)),
                  pl.BlockSpec((POOL_HALO, gsz), lambda i, g: (jnp.maximum(i * hb - 1, 0), g)),
                  pl.BlockSpec((1, gsz, gsz), lambda i, g: (g, 0, 0)),
                  pl.BlockSpec((1, 1, gsz), lambda i, g: (g, 0, 0))],
        out_specs=pl.BlockSpec((tp, gsz), lambda i, g: (i, g)),
        compiler_params=_params(("parallel", "arbitrary")),
        name="pool_mixer",
    )(act, act, w_group, scale.reshape(ng, 1, gsz))


def _level_ids():
    t = np.arange(CHUNK)[:, None]
    s = np.arange(CHUNK)[None, :]
    ids = -np.ones((CHUNK, CHUNK), np.int32)
    ids[t == s] = 0
    for lid, n in enumerate((1,) + tuple(reversed(LEVELS)), start=1):
        m = (t // (2 * n) == s // (2 * n)) & (t % (2 * n) >= n) & (s % (2 * n) < n)
        ids[m] = lid
    return ids


def _nt_dot(a, b):
    return lax.dot_general(a, b, (((1,), (1,)), ((), ())), preferred_element_type=F32)


def _tn_dot(a, b):
    return lax.dot_general(a, b, (((0,), (0,)), ((), ())), preferred_element_type=F32)


def _hgrn_kernel(q_ref, f_ref, v_ref, og_ref, gn_ref, ids_ref, o_ref, st_ref, b_ref, *, nh, nchunk):
    @pl.when(pl.program_id(1) == 0)
    def _():
        st_ref[...] = jnp.zeros_like(st_ref)

    ids = ids_ref[...]
    sub = lax.broadcasted_iota(jnp.int32, (8, HEAD_DIM), 0)
    tril = (lax.broadcasted_iota(jnp.int32, (CHUNK, CHUNK), 0)
            >= lax.broadcasted_iota(jnp.int32, (CHUNK, CHUNK), 1)).astype(BF16)

    def bcast_row(h, r, n):
        return jnp.broadcast_to(b_ref[h, r:r + 1, :], (n, HEAD_DIM))

    def neg_abs(x):
        return pltpu.bitcast(pltpu.bitcast(x, jnp.uint32) | jnp.uint32(0x80000000), F32)

    def chunk_body(c, carry):
        r0 = pl.multiple_of(c * CHUNK, CHUNK)
        heads = range(nh)
        rows = pl.ds(r0, CHUNK)
        lanes = [slice(h * HEAD_DIM, (h + 1) * HEAD_DIM) for h in heads]
        q = [q_ref[rows, lanes[h]] for h in heads]
        f = [f_ref[rows, lanes[h]] for h in heads]
        v = [v_ref[rows, lanes[h]].astype(BF16) for h in heads]
        k = [1.0 - f[h] for h in heads]

        b = []
        for h in heads:
            g = jnp.log(f[h]) * LOG2_E
            g_hi = g.astype(BF16)
            g_r = g - g_hi.astype(F32)
            g_mid = g_r.astype(BF16)
            g_lo = (g_r - g_mid.astype(F32)).astype(BF16)
            b.append(jnp.dot(tril, g_hi, preferred_element_type=F32)
                     + jnp.dot(tril, g_mid, preferred_element_type=F32)
                     + jnp.dot(tril, g_lo, preferred_element_type=F32))
        for h in heads:
            b_ref[h] = b[h]

        k_bf = [k[h].astype(BF16) for h in heads]
        scores = [jnp.where(ids == 0, _nt_dot(q[h].astype(BF16), k_bf[h]), 0.0) for h in heads]
        scores = [jnp.where(ids == 1, _nt_dot((q[h] * f[h]).astype(BF16), k_bf[h]), scores[h]) for h in heads]
        for lid, n in enumerate(reversed(LEVELS), start=2):
            for h in heads:
                if n == 2:
                    parts = [jnp.where(sub < 4, bcast_row(h, 8 * i + 1, 8), bcast_row(h, 8 * i + 5, 8))
                             for i in range(CHUNK // 8)]
                else:
                    parts = [bcast_row(h, blk * 2 * n + n - 1, 2 * n) for blk in range(CHUNK // (2 * n))]
                b_mid = parts[0] if len(parts) == 1 else jnp.concatenate(parts, axis=0)
                e = jnp.exp2(neg_abs(b[h] - b_mid))
                p = _nt_dot((q[h] * e).astype(BF16), (k[h] * e).astype(BF16))
                scores[h] = jnp.where(ids == lid, p, scores[h])

        o = []
        for h in heads:
            b_end = bcast_row(h, CHUNK - 1, CHUNK)
            st = st_ref[h]
            oh = jnp.dot(scores[h].astype(BF16), v[h], preferred_element_type=F32)
            o.append(oh + _nt_dot((q[h] * jnp.exp2(b[h])).astype(BF16), st.astype(BF16)))
            k_dec = (k[h] * jnp.exp2(b_end - b[h])).astype(BF16)
            st_ref[h] = st * jnp.exp2(b_end[:1]) + _tn_dot(v[h], k_dec)

        for h in heads:
            y = o[h] * _rms_scale(o[h]) * gn_ref[:, lanes[h]] * og_ref[rows, lanes[h]]
            o_ref[rows, lanes[h]] = y.astype(o_ref.dtype)
        return carry

    lax.fori_loop(0, nchunk, chunk_body, 0)


def _hgrn(act, gnorm, d):
    t = act.shape[0]
    width = d // 2
    nh = min(HGRN_HEADS_PER_STEP, width // HEAD_DIM)
    bw = nh * HEAD_DIM
    ngroups = width // bw
    tr = _tile(t, 512, CHUNK)

    def seg_spec(seg):
        return pl.BlockSpec((tr, bw), lambda g, r, seg=seg: (r, seg * ngroups + g))

    return pl.pallas_call(
        functools.partial(_hgrn_kernel, nh=nh, nchunk=tr // CHUNK),
        out_shape=jax.ShapeDtypeStruct((t, width), BF16),
        grid=(ngroups, t // tr),
        in_specs=[seg_spec(1), seg_spec(2), seg_spec(3), seg_spec(4),
                  pl.BlockSpec((1, bw), lambda g, r: (0, g)),
                  pl.BlockSpec((CHUNK, CHUNK), lambda g, r: (0, 0))],
        out_specs=pl.BlockSpec((tr, bw), lambda g, r: (r, g)),
        scratch_shapes=[pltpu.VMEM((nh, HEAD_DIM, HEAD_DIM), F32),
                        pltpu.VMEM((nh, CHUNK, HEAD_DIM), F32)],
        compiler_params=_params(("parallel", "arbitrary")),
        name="hgrn2",
    )(act, act, act, act, gnorm.reshape(1, width), jnp.asarray(_level_ids()))


def _merge_kernel(yp_ref, yh_ref, wp_ref, wh_ref, ga_ref, gb_ref, o_ref):
    a = jnp.dot(yp_ref[...], wp_ref[...].astype(BF16), preferred_element_type=F32)
    b = jnp.dot(yh_ref[...], wh_ref[...].astype(BF16), preferred_element_type=F32)
    o_ref[...] = (ga_ref[...] * a + gb_ref[...] * b).astype(o_ref.dtype)


def _merge(y_pool, y_hgrn, w_up_pool, w_up_hgrn, act, d):
    t, kp = y_pool.shape
    kh = y_hgrn.shape[1]
    tm = _tile(t, 2048, 8)
    tn = _tile(d // 2, 256)
    nb = d // tn
    off_a = (act.shape[1] - 2 * d) // tn
    return pl.pallas_call(
        _merge_kernel,
        out_shape=jax.ShapeDtypeStruct((t, d), BF16),
        grid=(t // tm, nb),
        in_specs=[pl.BlockSpec((tm, kp), lambda i, j: (i, 0)),
                  pl.BlockSpec((tm, kh), lambda i, j: (i, 0)),
                  pl.BlockSpec((kp, tn), lambda i, j: (0, j)),
                  pl.BlockSpec((kh, tn), lambda i, j: (0, j)),
                  pl.BlockSpec((tm, tn), lambda i, j: (i, off_a + j)),
                  pl.BlockSpec((tm, tn), lambda i, j: (i, off_a + nb + j))],
        out_specs=pl.BlockSpec((tm, tn), lambda i, j: (i, j)),
        compiler_params=_params(("parallel", "arbitrary")),
        name="gated_merge",
    )(y_pool, y_hgrn, w_up_pool, w_up_hgrn, act, act)


def _outproj_kernel(m_ref, w_ref, x_ref, h_ref):
    h_ref[...] = x_ref[...] + jnp.dot(m_ref[...], w_ref[...].astype(BF16), preferred_element_type=F32)


def _outproj(m, w, x):
    t, d = x.shape
    tm = _tile(t, 2048, 8)
    tn = _tile(d, 256)
    return pl.pallas_call(
        _outproj_kernel,
        out_shape=jax.ShapeDtypeStruct((t, d), F32),
        grid=(t // tm, d // tn),
        in_specs=[pl.BlockSpec((tm, d), lambda i, j: (i, 0)),
                  pl.BlockSpec((d, tn), lambda i, j: (0, j)),
                  pl.BlockSpec((tm, tn), lambda i, j: (i, j))],
        out_specs=pl.BlockSpec((tm, tn), lambda i, j: (i, j)),
        compiler_params=_params(("parallel", "arbitrary")),
        name="out_proj",
    )(m, w, x)


def _ffn_kernel(h_hbm, wg_ref, wu_ref, wd_ref, gin_ref, gout_ref, o_hbm, acc_ref, v_ref, sem,
                *, nj, tm, final_norm):
    i = pl.program_id(0)
    j = pl.program_id(1)
    chunk = min(tm, NORM_ROWS)
    starts = range(0, tm, chunk)

    def copy(c, r, to_vmem):
        hbm = (h_hbm if to_vmem else o_hbm).at[pl.ds(pl.multiple_of(i * tm, tm) + r, chunk), :]
        vmem = acc_ref.at[r:r + chunk, :]
        return pltpu.make_async_copy(hbm, vmem, sem.at[c]) if to_vmem else \
            pltpu.make_async_copy(vmem, hbm, sem.at[c])

    @pl.when(j == 0)
    def _():
        for c, r in enumerate(starts):
            copy(c, r, True).start()
        for c, r in enumerate(starts):
            copy(c, r, True).wait()
            h = acc_ref[r:r + chunk, :]
            v_ref[r:r + chunk, :] = (h * _rms_scale(h) * gin_ref[...]).astype(BF16)

    v = v_ref[...]
    gate = jnp.dot(v, wg_ref[...].astype(BF16), preferred_element_type=F32)
    up = jnp.dot(v, wu_ref[...].astype(BF16), preferred_element_type=F32)
    a = (gate * _sigmoid(gate) * up).astype(BF16)
    acc_ref[...] += jnp.dot(a, wd_ref[...].astype(BF16), preferred_element_type=F32)

    @pl.when(j == nj - 1)
    def _():
        for c, r in enumerate(starts):
            if final_norm:
                h = acc_ref[r:r + chunk, :]
                acc_ref[r:r + chunk, :] = h * _rms_scale(h) * gout_ref[...]
            copy(c, r, False).start()
        for c, r in enumerate(starts):
            copy(c, r, False).wait()


def _ffn(h, wg, wu, wd, g_in, g_out, final_norm):
    t, d = h.shape
    ff = wd.shape[0]
    tm = _tile(t, 1024, 8)
    tf = _tile(ff, 256)
    nj = ff // tf
    return pl.pallas_call(
        functools.partial(_ffn_kernel, nj=nj, tm=tm, final_norm=final_norm),
        out_shape=jax.ShapeDtypeStruct((t, d), F32),
        grid=(t // tm, nj),
        in_specs=[pl.BlockSpec(memory_space=pl.ANY),
                  pl.BlockSpec((d, tf), lambda i, j: (0, j)),
                  pl.BlockSpec((d, tf), lambda i, j: (0, j)),
                  pl.BlockSpec((tf, d), lambda i, j: (j, 0)),
                  pl.BlockSpec((1, d), lambda i, j: (0, 0)),
                  pl.BlockSpec((1, d), lambda i, j: (0, 0))],
        out_specs=pl.BlockSpec(memory_space=pl.ANY),
        scratch_shapes=[pltpu.VMEM((tm, d), F32), pltpu.VMEM((tm, d), BF16),
                        pltpu.SemaphoreType.DMA((pl.cdiv(tm, min(tm, NORM_ROWS)),))],
        compiler_params=_params(("arbitrary", "arbitrary")),
        name="swiglu_ffn",
    )(h, wg, wu, wd, g_in.reshape(1, d), g_out.reshape(1, d))


def kernel(x, g_mix, w_in, w_pool_group, pool_scale, lb_param, hgrn_norm, w_up_pool, w_up_hgrn, w_out,
           g_ffn, w_ffn_gate, w_ffn_up, w_ffn_down, g_final):
    bsz, t, d = x.shape
    depth = g_mix.shape[0]
    assert d % (2 * HEAD_DIM) == 0 and t % CHUNK == 0
    outs = []
    for bi in range(bsz):
        h = x[bi]
        for l in range(depth):
            u = _rmsnorm(h, g_mix[l], BF16)
            act = _proj(u, w_in[l], lb_param, l)
            y_pool = _pool(act, w_pool_group[l], pool_scale[l])
            y_hgrn = _hgrn(act, hgrn_norm[l], d)
            merged = _merge(y_pool, y_hgrn, w_up_pool[l], w_up_hgrn[l], act, d)
            h1 = _outproj(merged, w_out[l], h)
            h = _ffn(h1, w_ffn_gate[l], w_ffn_up[l], w_ffn_down[l], g_ffn[l], g_final, l == depth - 1)
        outs.append(h)
    return jnp.stack(outs, axis=0)
```

```python
import functools

import numpy as np
import jax
import jax.numpy as jnp
from jax import lax
from jax.experimental import pallas as pl
from jax.experimental.pallas import tpu as pltpu

EPS = 1e-6
POOL_WINDOWS = (2, 4, 8, 16)
POOL_HALO = 16
HEAD_DIM = 128
CHUNK = 64
NORM_ROWS = 256
PROJ_ROW_SPLITS = 4
HGRN_HEADS_PER_STEP = 16
LEVELS = (32, 16, 8, 4, 2)

LOG2_E = 1.4426950408889634

F32 = jnp.float32
BF16 = jnp.bfloat16

VMEM_LIMIT_BYTES = 60 * 1024 * 1024


def _tile(dim, pref, align=128):
    if dim <= pref:
        return dim
    t = (pref // align) * align
    while t >= align:
        if dim % t == 0:
            return t
        t -= align
    return dim


def _params(sem, flags=None):
    return pltpu.CompilerParams(dimension_semantics=sem, vmem_limit_bytes=VMEM_LIMIT_BYTES, flags=flags)


def _sigmoid(x):
    return 1.0 / (1.0 + jnp.exp(-x))


def _rms_scale(h):
    return lax.rsqrt(jnp.mean(h * h, axis=-1, keepdims=True) + EPS)


def _rmsnorm_kernel(x_ref, g_ref, o_ref):
    x = x_ref[...]
    o_ref[...] = (x * _rms_scale(x) * g_ref[...]).astype(o_ref.dtype)


def _rmsnorm(x, g, out_dtype):
    t, d = x.shape
    tm = _tile(t, 256, 8)
    return pl.pallas_call(
        _rmsnorm_kernel,
        out_shape=jax.ShapeDtypeStruct((t, d), out_dtype),
        grid=(t // tm,),
        in_specs=[pl.BlockSpec((tm, d), lambda i: (i, 0)),
                  pl.BlockSpec((1, d), lambda i: (0, 0))],
        out_specs=pl.BlockSpec((tm, d), lambda i: (i, 0)),
        compiler_params=_params(("parallel",)),
        name="rmsnorm",
    )(x, g.reshape(1, d))


def _proj_kernel(u_ref, w_ref, lbp_ref, o_ref, *, layer, tn, half):
    seg = (pl.program_id(1) * tn) // half
    p = lbp_ref[...]
    e = jnp.exp(p - jnp.max(p, axis=0, keepdims=True))
    lb = jnp.sum(e[: layer + 1], axis=0, keepdims=True) / jnp.sum(e, axis=0, keepdims=True)
    is_f = seg == 2
    is_silu = jnp.logical_or(seg == 1, seg == 4)
    c0 = jnp.where(is_f, lb, 0.0)
    c1 = jnp.where(jnp.logical_or(seg == 0, seg == 3), 1.0, 0.0)
    c2 = jnp.where(is_f, 1.0 - lb, jnp.where(seg >= 5, 1.0, 0.0))
    c3 = jnp.where(is_silu, 1.0, 0.0)
    w = w_ref[...].astype(BF16)
    tm = u_ref.shape[0]
    rows = tm // PROJ_ROW_SPLITS
    for r in range(0, tm, rows):
        acc = jnp.dot(u_ref[r:r + rows, :], w, preferred_element_type=F32)
        o_ref[r:r + rows, :] = c0 + c1 * acc + _sigmoid(acc) * (c2 + c3 * acc)


def _proj(u, w, lb_param, layer):
    t, d = u.shape
    n_in = w.shape[1]
    half = d // 2
    tm = _tile(t, 2048, 8)
    tn = _tile(half, 512)
    nb = half // tn
    return pl.pallas_call(
        functools.partial(_proj_kernel, layer=layer, tn=tn, half=half),
        out_shape=jax.ShapeDtypeStruct((t, n_in), F32),
        grid=(t // tm, n_in // tn),
        in_specs=[pl.BlockSpec((tm, d), lambda i, j: (i, 0), pipeline_mode=pl.Buffered(1)),
                  pl.BlockSpec((d, tn), lambda i, j: (0, j)),
                  pl.BlockSpec((lb_param.shape[0], tn), lambda i, j: (0, jnp.clip(j - 2 * nb, 0, nb - 1)))],
        out_specs=pl.BlockSpec((tm, tn), lambda i, j: (i, j)),
        compiler_params=_params(("parallel", "arbitrary")),
        name="in_proj",
    )(u, w, lb_param)


def _pool_kernel(z_ref, halo_ref, w_ref, s_ref, o_ref, *, tp):
    i = pl.program_id(0)
    g = pl.program_id(1)
    z = z_ref[...]
    halo = jnp.where(i > 0, halo_ref[...], 0.0)
    full = jnp.concatenate([halo, z], axis=0)
    t = i * tp + lax.broadcasted_iota(jnp.int32, (tp, 1), 0)

    for gi, w in enumerate(POOL_WINDOWS):
        @pl.when(g == gi)
        def _(w=w):
            s = full
            k = 1
            while k < w:
                s = s + pltpu.roll(s, k, 0)
                k *= 2
            count = jnp.minimum(t + 1, w).astype(F32)
            pooled = s[POOL_HALO:] / count - z
            y = jnp.dot(pooled.astype(BF16), w_ref[0].astype(BF16), preferred_element_type=F32)
            o_ref[...] = (y * s_ref[0]).astype(o_ref.dtype)


def _pool(act, w_group, scale):
    t = act.shape[0]
    ng, gsz, _ = w_group.shape
    assert ng == len(POOL_WINDOWS)
    tp = _tile(t, 2048, POOL_HALO)
    hb = tp // POOL_HALO
    return pl.pallas_call(
        functools.partial(_pool_kernel, tp=tp),
        out_shape=jax.ShapeDtypeStruct((t, ng * gsz), BF16),
        grid=(t // tp, ng),
        in_specs=[pl.BlockSpec((tp, gsz), lambda i, g: (i, g)),
                  pl.BlockSpec((POOL_HALO, gsz), lambda i, g: (jnp.maximum(i * hb - 1, 0), g)),
                  pl.BlockSpec((1, gsz, gsz), lambda i, g: (g, 0, 0)),
                  pl.BlockSpec((1, 1, gsz), lambda i, g: (g, 0, 0))],
        out_specs=pl.BlockSpec((tp, gsz), lambda i, g: (i, g)),
        compiler_params=_params(("parallel", "arbitrary")),
        name="pool_mixer",
    )(act, act, w_group, scale.reshape(ng, 1, gsz))


def _level_ids():
    t = np.arange(CHUNK)[:, None]
    s = np.arange(CHUNK)[None, :]
    ids = -np.ones((CHUNK, CHUNK), np.int32)
    ids[t == s] = 0
    for lid, n in enumerate((1,) + tuple(reversed(LEVELS)), start=1):
        m = (t // (2 * n) == s // (2 * n)) & (t % (2 * n) >= n) & (s % (2 * n) < n)
        ids[m] = lid
    return ids


def _nt_dot(a, b):
    return lax.dot_general(a, b, (((1,), (1,)), ((), ())), preferred_element_type=F32)


def _tn_dot(a, b):
    return lax.dot_general(a, b, (((0,), (0,)), ((), ())), preferred_element_type=F32)


def _hgrn_kernel(q_ref, f_ref, v_ref, og_ref, gn_ref, ids_ref, o_ref, st_ref, b_ref, *, nh, nchunk):
    @pl.when(pl.program_id(1) == 0)
    def _():
        st_ref[...] = jnp.zeros_like(st_ref)

    ids = ids_ref[...]
    sub = lax.broadcasted_iota(jnp.int32, (8, HEAD_DIM), 0)
    row = lax.broadcasted_iota(jnp.int32, (CHUNK, HEAD_DIM), 0)
    tril = (lax.broadcasted_iota(jnp.int32, (CHUNK, CHUNK), 0)
            >= lax.broadcasted_iota(jnp.int32, (CHUNK, CHUNK), 1)).astype(BF16)

    def bcast_row(h, r, n):
        return jnp.broadcast_to(b_ref[h, r:r + 1, :], (n, HEAD_DIM))

    def neg_abs(x):
        return pltpu.bitcast(pltpu.bitcast(x, jnp.uint32) | jnp.uint32(0x80000000), F32)

    def chunk_body(c, carry):
        r0 = pl.multiple_of(c * CHUNK, CHUNK)
        heads = range(nh)
        rows = pl.ds(r0, CHUNK)
        lanes = [slice(h * HEAD_DIM, (h + 1) * HEAD_DIM) for h in heads]
        q = [q_ref[rows, lanes[h]] for h in heads]
        f = [f_ref[rows, lanes[h]] for h in heads]
        v = [v_ref[rows, lanes[h]].astype(BF16) for h in heads]
        k = [1.0 - f[h] for h in heads]

        b = []
        for h in heads:
            g = jnp.log(f[h]) * LOG2_E
            g_hi = g.astype(BF16)
            g_r = g - g_hi.astype(F32)
            g_mid = g_r.astype(BF16)
            g_lo = (g_r - g_mid.astype(F32)).astype(BF16)
            b.append(jnp.dot(tril, g_hi, preferred_element_type=F32)
                     + jnp.dot(tril, g_mid, preferred_element_type=F32)
                     + jnp.dot(tril, g_lo, preferred_element_type=F32))
        for h in heads:
            b_ref[h] = b[h]

        k_bf = [k[h].astype(BF16) for h in heads]
        scores = [jnp.where(ids == 0, _nt_dot(q[h].astype(BF16), k_bf[h]), 0.0) for h in heads]
        scores = [jnp.where(ids == 1, _nt_dot((q[h] * f[h]).astype(BF16), k_bf[h]), scores[h]) for h in heads]
        for lid, n in enumerate(reversed(LEVELS), start=2):
            for h in heads:
                if n >= 8:
                    d_parts, x_parts = [], []
                    for base in range(0, CHUNK, 2 * n):
                        lo, up = slice(base, base + n), slice(base + n, base + 2 * n)
                        mid = bcast_row(h, base + n - 1, n)
                        d_parts += [mid - b[h][lo], b[h][up] - mid]
                        x_parts += [k[h][lo], q[h][up]]
                    d = jnp.concatenate(d_parts, axis=0)
                    xs = jnp.concatenate(x_parts, axis=0)
                else:
                    if n == 4:
                        parts = [bcast_row(h, 8 * i + 3, 8) for i in range(CHUNK // 8)]
                    else:
                        parts = [jnp.where(sub < 4, bcast_row(h, 8 * i + 1, 8), bcast_row(h, 8 * i + 5, 8))
                                 for i in range(CHUNK // 8)]
                    d = neg_abs(b[h] - jnp.concatenate(parts, axis=0))
                    xs = jnp.where((row & n) != 0, q[h], k[h])
                x = (xs * jnp.exp2(d)).astype(BF16)
                scores[h] = jnp.where(ids == lid, _nt_dot(x, x), scores[h])

        o = []
        for h in heads:
            b_end = bcast_row(h, CHUNK - 1, CHUNK)
            st = st_ref[h]
            oh = jnp.dot(scores[h].astype(BF16), v[h], preferred_element_type=F32)
            o.append(oh + _nt_dot((q[h] * jnp.exp2(b[h])).astype(BF16), st.astype(BF16)))
            k_dec = (k[h] * jnp.exp2(b_end - b[h])).astype(BF16)
            st_ref[h] = st * jnp.exp2(b_end[:1]) + _tn_dot(v[h], k_dec)

        for h in heads:
            y = o[h] * _rms_scale(o[h]) * gn_ref[:, lanes[h]] * og_ref[rows, lanes[h]]
            o_ref[rows, lanes[h]] = y.astype(o_ref.dtype)
        return carry

    lax.fori_loop(0, nchunk, chunk_body, 0)


def _hgrn(act, gnorm, d):
    t = act.shape[0]
    width = d // 2
    nh = min(HGRN_HEADS_PER_STEP, width // HEAD_DIM)
    bw = nh * HEAD_DIM
    ngroups = width // bw
    tr = _tile(t, 512, CHUNK)

    def seg_spec(seg):
        return pl.BlockSpec((tr, bw), lambda g, r, seg=seg: (r, seg * ngroups + g))

    return pl.pallas_call(
        functools.partial(_hgrn_kernel, nh=nh, nchunk=tr // CHUNK),
        out_shape=jax.ShapeDtypeStruct((t, width), BF16),
        grid=(ngroups, t // tr),
        in_specs=[seg_spec(1), seg_spec(2), seg_spec(3), seg_spec(4),
                  pl.BlockSpec((1, bw), lambda g, r: (0, g)),
                  pl.BlockSpec((CHUNK, CHUNK), lambda g, r: (0, 0))],
        out_specs=pl.BlockSpec((tr, bw), lambda g, r: (r, g)),
        scratch_shapes=[pltpu.VMEM((nh, HEAD_DIM, HEAD_DIM), F32),
                        pltpu.VMEM((nh, CHUNK, HEAD_DIM), F32)],
        compiler_params=_params(("parallel", "arbitrary")),
        name="hgrn2",
    )(act, act, act, act, gnorm.reshape(1, width), jnp.asarray(_level_ids()))


def _merge_kernel(yp_ref, yh_ref, wp_ref, wh_ref, ga_ref, gb_ref, o_ref):
    a = jnp.dot(yp_ref[...], wp_ref[...].astype(BF16), preferred_element_type=F32)
    b = jnp.dot(yh_ref[...], wh_ref[...].astype(BF16), preferred_element_type=F32)
    o_ref[...] = (ga_ref[...] * a + gb_ref[...] * b).astype(o_ref.dtype)


def _merge(y_pool, y_hgrn, w_up_pool, w_up_hgrn, act, d):
    t, kp = y_pool.shape
    kh = y_hgrn.shape[1]
    tm = _tile(t, 2048, 8)
    tn = _tile(d // 2, 256)
    nb = d // tn
    off_a = (act.shape[1] - 2 * d) // tn
    return pl.pallas_call(
        _merge_kernel,
        out_shape=jax.ShapeDtypeStruct((t, d), BF16),
        grid=(t // tm, nb),
        in_specs=[pl.BlockSpec((tm, kp), lambda i, j: (i, 0)),
                  pl.BlockSpec((tm, kh), lambda i, j: (i, 0)),
                  pl.BlockSpec((kp, tn), lambda i, j: (0, j)),
                  pl.BlockSpec((kh, tn), lambda i, j: (0, j)),
                  pl.BlockSpec((tm, tn), lambda i, j: (i, off_a + j)),
                  pl.BlockSpec((tm, tn), lambda i, j: (i, off_a + nb + j))],
        out_specs=pl.BlockSpec((tm, tn), lambda i, j: (i, j)),
        compiler_params=_params(("parallel", "arbitrary")),
        name="gated_merge",
    )(y_pool, y_hgrn, w_up_pool, w_up_hgrn, act, act)


def _outproj_kernel(m_ref, w_ref, x_ref, h_ref):
    h_ref[...] = x_ref[...] + jnp.dot(m_ref[...], w_ref[...].astype(BF16), preferred_element_type=F32)


def _outproj(m, w, x):
    t, d = x.shape
    tm = _tile(t, 2048, 8)
    tn = _tile(d, 256)
    return pl.pallas_call(
        _outproj_kernel,
        out_shape=jax.ShapeDtypeStruct((t, d), F32),
        grid=(t // tm, d // tn),
        in_specs=[pl.BlockSpec((tm, d), lambda i, j: (i, 0)),
                  pl.BlockSpec((d, tn), lambda i, j: (0, j)),
                  pl.BlockSpec((tm, tn), lambda i, j: (i, j))],
        out_specs=pl.BlockSpec((tm, tn), lambda i, j: (i, j)),
        compiler_params=_params(("parallel", "arbitrary")),
        name="out_proj",
    )(m, w, x)


def _ffn_kernel(h_hbm, wg_ref, wu_ref, wd_ref, gin_ref, gout_ref, o_hbm, acc_ref, v_ref, sem,
                *, nj, tm, final_norm):
    i = pl.program_id(0)
    j = pl.program_id(1)
    chunk = min(tm, NORM_ROWS)
    starts = range(0, tm, chunk)

    def copy(c, r, to_vmem):
        hbm = (h_hbm if to_vmem else o_hbm).at[pl.ds(pl.multiple_of(i * tm, tm) + r, chunk), :]
        vmem = acc_ref.at[r:r + chunk, :]
        return pltpu.make_async_copy(hbm, vmem, sem.at[c]) if to_vmem else \
            pltpu.make_async_copy(vmem, hbm, sem.at[c])

    @pl.when(j == 0)
    def _():
        for c, r in enumerate(starts):
            copy(c, r, True).start()
        for c, r in enumerate(starts):
            copy(c, r, True).wait()
            h = acc_ref[r:r + chunk, :]
            v_ref[r:r + chunk, :] = (h * _rms_scale(h) * gin_ref[...]).astype(BF16)

    v = v_ref[...]
    gate = jnp.dot(v, wg_ref[...].astype(BF16), preferred_element_type=F32)
    up = jnp.dot(v, wu_ref[...].astype(BF16), preferred_element_type=F32)
    a = (gate * _sigmoid(gate) * up).astype(BF16)
    acc_ref[...] += jnp.dot(a, wd_ref[...].astype(BF16), preferred_element_type=F32)

    @pl.when(j == nj - 1)
    def _():
        for c, r in enumerate(starts):
            if final_norm:
                h = acc_ref[r:r + chunk, :]
                acc_ref[r:r + chunk, :] = h * _rms_scale(h) * gout_ref[...]
            copy(c, r, False).start()
        for c, r in enumerate(starts):
            copy(c, r, False).wait()


def _ffn(h, wg, wu, wd, g_in, g_out, final_norm):
    t, d = h.shape
    ff = wd.shape[0]
    tm = _tile(t, 1024, 8)
    tf = _tile(ff, 256)
    nj = ff // tf
    return pl.pallas_call(
        functools.partial(_ffn_kernel, nj=nj, tm=tm, final_norm=final_norm),
        out_shape=jax.ShapeDtypeStruct((t, d), F32),
        grid=(t // tm, nj),
        in_specs=[pl.BlockSpec(memory_space=pl.ANY),
                  pl.BlockSpec((d, tf), lambda i, j: (0, j)),
                  pl.BlockSpec((d, tf), lambda i, j: (0, j)),
                  pl.BlockSpec((tf, d), lambda i, j: (j, 0)),
                  pl.BlockSpec((1, d), lambda i, j: (0, 0)),
                  pl.BlockSpec((1, d), lambda i, j: (0, 0))],
        out_specs=pl.BlockSpec(memory_space=pl.ANY),
        scratch_shapes=[pltpu.VMEM((tm, d), F32), pltpu.VMEM((tm, d), BF16),
                        pltpu.SemaphoreType.DMA((pl.cdiv(tm, min(tm, NORM_ROWS)),))],
        compiler_params=_params(("arbitrary", "arbitrary")),
        name="swiglu_ffn",
    )(h, wg, wu, wd, g_in.reshape(1, d), g_out.reshape(1, d))


def kernel(x, g_mix, w_in, w_pool_group, pool_scale, lb_param, hgrn_norm, w_up_pool, w_up_hgrn, w_out,
           g_ffn, w_ffn_gate, w_ffn_up, w_ffn_down, g_final):
    bsz, t, d = x.shape
    depth = g_mix.shape[0]
    assert d % (2 * HEAD_DIM) == 0 and t % CHUNK == 0
    outs = []
    for bi in range(bsz):
        h = x[bi]
        for l in range(depth):
            u = _rmsnorm(h, g_mix[l], BF16)
            act = _proj(u, w_in[l], lb_param, l)
            y_pool = _pool(act, w_pool_group[l], pool_scale[l])
            y_hgrn = _hgrn(act, hgrn_norm[l], d)
            merged = _merge(y_pool, y_hgrn, w_up_pool[l], w_up_hgrn[l], act, d)
            h1 = _outproj(merged, w_out[l], h)
            h = _ffn(h1, w_ffn_gate[l], w_ffn_up[l], w_ffn_down[l], g_ffn[l], g_final, l == depth - 1)
        outs.append(h)
    return jnp.stack(outs, axis=0)
```

```python
import functools

import numpy as np
import jax
import jax.numpy as jnp
from jax import lax
from jax.experimental import pallas as pl
from jax.experimental.pallas import tpu as pltpu

EPS = 1e-6
POOL_WINDOWS = (2, 4, 8, 16)
POOL_HALO = 16
HEAD_DIM = 128
CHUNK = 64
NORM_ROWS = 256
PROJ_ROW_SPLITS = 4
HGRN_HEADS_PER_STEP = 16
HGRN_CHUNKS_PER_ITER = 2
LEVELS = (32, 16, 8, 4, 2)

LOG2_E = 1.4426950408889634

F32 = jnp.float32
BF16 = jnp.bfloat16

VMEM_LIMIT_BYTES = 60 * 1024 * 1024


def _tile(dim, pref, align=128):
    if dim <= pref:
        return dim
    t = (pref // align) * align
    while t >= align:
        if dim % t == 0:
            return t
        t -= align
    return dim


def _params(sem, flags=None):
    return pltpu.CompilerParams(dimension_semantics=sem, vmem_limit_bytes=VMEM_LIMIT_BYTES, flags=flags)


def _sigmoid(x):
    return 1.0 / (1.0 + jnp.exp(-x))


def _rms_scale(h):
    return lax.rsqrt(jnp.mean(h * h, axis=-1, keepdims=True) + EPS)


def _rms_scale_ref(ref, r, n):
    half = ref.shape[1] // 2
    lo = ref[r:r + n, :half]
    hi = ref[r:r + n, half:]
    ss = jnp.sum(lo * lo, axis=-1, keepdims=True) + jnp.sum(hi * hi, axis=-1, keepdims=True)
    return lax.rsqrt(ss / ref.shape[1] + EPS)


def _proj_kernel(x_hbm, g_ref, w_ref, lbp_ref, o_ref, u_ref, xbuf, sem, *, layer, tn, half):
    i = pl.program_id(0)
    tm = u_ref.shape[0]
    rc = xbuf.shape[1]

    def xcopy(c):
        src = x_hbm.at[pl.ds(pl.multiple_of(i * tm, tm) + c * rc, rc), :]
        return pltpu.make_async_copy(src, xbuf.at[c % 2], sem.at[c % 2])

    @pl.when(pl.program_id(1) == 0)
    def _():
        nchunks = tm // rc
        for c in range(min(2, nchunks)):
            xcopy(c).start()
        for c in range(nchunks):
            xcopy(c).wait()
            slot = xbuf.at[c % 2]
            scale = _rms_scale_ref(slot, 0, rc)
            u_ref[c * rc:(c + 1) * rc, :] = (slot[...] * scale * g_ref[...]).astype(BF16)
            if c + 2 < nchunks:
                xcopy(c + 2).start()

    seg = (pl.program_id(1) * tn) // half
    p = lbp_ref[...]
    e = jnp.exp(p - jnp.max(p, axis=0, keepdims=True))
    lb = jnp.sum(e[: layer + 1], axis=0, keepdims=True) / jnp.sum(e, axis=0, keepdims=True)
    is_f = seg == 2
    is_silu = jnp.logical_or(seg == 1, seg == 4)
    c0 = jnp.where(is_f, lb, 0.0)
    c1 = jnp.where(jnp.logical_or(seg == 0, seg == 3), 1.0, 0.0)
    c2 = jnp.where(is_f, 1.0 - lb, jnp.where(seg >= 5, 1.0, 0.0))
    c3 = jnp.where(is_silu, 1.0, 0.0)
    w = w_ref[...].astype(BF16)
    rows = tm // PROJ_ROW_SPLITS
    for r in range(0, tm, rows):
        acc = jnp.dot(u_ref[r:r + rows, :], w, preferred_element_type=F32)
        o_ref[r:r + rows, :] = c0 + c1 * acc + _sigmoid(acc) * (c2 + c3 * acc)


def _proj(x, g, w, lb_param, layer):
    t, d = x.shape
    n_in = w.shape[1]
    half = d // 2
    tm = _tile(t, 2048, 8)
    tn = _tile(half, 512)
    rc = _tile(tm, NORM_ROWS, 8)
    nb = half // tn
    return pl.pallas_call(
        functools.partial(_proj_kernel, layer=layer, tn=tn, half=half),
        out_shape=jax.ShapeDtypeStruct((t, n_in), F32),
        grid=(t // tm, n_in // tn),
        in_specs=[pl.BlockSpec(memory_space=pl.ANY),
                  pl.BlockSpec((1, d), lambda i, j: (0, 0)),
                  pl.BlockSpec((d, tn), lambda i, j: (0, j)),
                  pl.BlockSpec((lb_param.shape[0], tn), lambda i, j: (0, jnp.clip(j - 2 * nb, 0, nb - 1)))],
        out_specs=pl.BlockSpec((tm, tn), lambda i, j: (i, j)),
        scratch_shapes=[pltpu.VMEM((tm, d), BF16), pltpu.VMEM((2, rc, d), F32), pltpu.SemaphoreType.DMA((2,))],
        compiler_params=_params(("arbitrary", "arbitrary")),
        name="in_proj",
    )(x, g.reshape(1, d), w, lb_param)


def _pool_kernel(z_ref, halo_ref, w_ref, s_ref, o_ref, *, tp):
    i = pl.program_id(0)
    g = pl.program_id(1)
    z = z_ref[...]
    halo = jnp.where(i > 0, halo_ref[...], 0.0)
    full = jnp.concatenate([halo, z], axis=0)
    t = i * tp + lax.broadcasted_iota(jnp.int32, (tp, 1), 0)

    for gi, w in enumerate(POOL_WINDOWS):
        @pl.when(g == gi)
        def _(w=w):
            s = full
            k = 1
            while k < w:
                s = s + pltpu.roll(s, k, 0)
                k *= 2
            count = jnp.minimum(t + 1, w).astype(F32)
            pooled = s[POOL_HALO:] / count - z
            y = jnp.dot(pooled.astype(BF16), w_ref[0].astype(BF16), preferred_element_type=F32)
            o_ref[...] = (y * s_ref[0]).astype(o_ref.dtype)


def _pool(act, w_group, scale):
    t = act.shape[0]
    ng, gsz, _ = w_group.shape
    assert ng == len(POOL_WINDOWS)
    tp = _tile(t, 2048, POOL_HALO)
    hb = tp // POOL_HALO
    return pl.pallas_call(
        functools.partial(_pool_kernel, tp=tp),
        out_shape=jax.ShapeDtypeStruct((t, ng * gsz), BF16),
        grid=(t // tp, ng),
        in_specs=[pl.BlockSpec((tp, gsz), lambda i, g: (i, g)),
                  pl.BlockSpec((POOL_HALO, gsz), lambda i, g: (jnp.maximum(i * hb - 1, 0), g)),
                  pl.BlockSpec((1, gsz, gsz), lambda i, g: (g, 0, 0)),
                  pl.BlockSpec((1, 1, gsz), lambda i, g: (g, 0, 0))],
        out_specs=pl.BlockSpec((tp, gsz), lambda i, g: (i, g)),
        compiler_params=_params(("parallel", "arbitrary")),
        name="pool_mixer",
    )(act, act, w_group, scale.reshape(ng, 1, gsz))


def _level_ids():
    t = np.arange(CHUNK)[:, None]
    s = np.arange(CHUNK)[None, :]
    ids = -np.ones((CHUNK, CHUNK), np.int32)
    ids[t == s] = 0
    for lid, n in enumerate((1,) + tuple(reversed(LEVELS)), start=1):
        m = (t // (2 * n) == s // (2 * n)) & (t % (2 * n) >= n) & (s % (2 * n) < n)
        ids[m] = lid
    return ids


def _nt_dot(a, b):
    return lax.dot_general(a, b, (((1,), (1,)), ((), ())), preferred_element_type=F32)


def _tn_dot(a, b):
    return lax.dot_general(a, b, (((0,), (0,)), ((), ())), preferred_element_type=F32)


def _hgrn_kernel(q_ref, f_ref, v_ref, og_ref, gn_ref, ids_ref, o_ref, st_ref, b_ref, *, nh, nchunk):
    @pl.when(pl.program_id(1) == 0)
    def _():
        st_ref[...] = jnp.zeros_like(st_ref)

    ids = ids_ref[...]
    sub = lax.broadcasted_iota(jnp.int32, (8, HEAD_DIM), 0)
    row = lax.broadcasted_iota(jnp.int32, (CHUNK, HEAD_DIM), 0)
    tril = (lax.broadcasted_iota(jnp.int32, (CHUNK, CHUNK), 0)
            >= lax.broadcasted_iota(jnp.int32, (CHUNK, CHUNK), 1)).astype(BF16)

    def bcast_row(h, r, n):
        return jnp.broadcast_to(b_ref[h, r:r + 1, :], (n, HEAD_DIM))

    def neg_abs(x):
        return pltpu.bitcast(pltpu.bitcast(x, jnp.uint32) | jnp.uint32(0x80000000), F32)

    def chunk_body(c, carry):
        heads = range(nh * HGRN_CHUNKS_PER_ITER)
        rows = [pl.ds(pl.multiple_of((c * HGRN_CHUNKS_PER_ITER + h // nh) * CHUNK, CHUNK), CHUNK) for h in heads]
        lanes = [slice((h % nh) * HEAD_DIM, (h % nh + 1) * HEAD_DIM) for h in heads]
        q = [q_ref[rows[h], lanes[h]] for h in heads]
        f = [f_ref[rows[h], lanes[h]] for h in heads]
        v = [v_ref[rows[h], lanes[h]].astype(BF16) for h in heads]
        k = [1.0 - f[h] for h in heads]

        b = []
        for h in heads:
            g = jnp.log(f[h]) * LOG2_E
            g_hi = g.astype(BF16)
            g_r = g - g_hi.astype(F32)
            g_mid = g_r.astype(BF16)
            g_lo = (g_r - g_mid.astype(F32)).astype(BF16)
            b.append(jnp.dot(tril, g_hi, preferred_element_type=F32)
                     + jnp.dot(tril, g_mid, preferred_element_type=F32)
                     + jnp.dot(tril, g_lo, preferred_element_type=F32))
        for h in heads:
            b_ref[h] = b[h]

        k_bf = [k[h].astype(BF16) for h in heads]
        scores = [jnp.where(ids == 0, _nt_dot(q[h].astype(BF16), k_bf[h]), 0.0) for h in heads]
        scores = [jnp.where(ids == 1, _nt_dot((q[h] * f[h]).astype(BF16), k_bf[h]), scores[h]) for h in heads]
        for lid, n in enumerate(reversed(LEVELS), start=2):
            for h in heads:
                if n >= 8:
                    d_parts, x_parts = [], []
                    for base in range(0, CHUNK, 2 * n):
                        lo, up = slice(base, base + n), slice(base + n, base + 2 * n)
                        mid = bcast_row(h, base + n - 1, n)
                        d_parts += [mid - b[h][lo], b[h][up] - mid]
                        x_parts += [k[h][lo], q[h][up]]
                    d = jnp.concatenate(d_parts, axis=0)
                    xs = jnp.concatenate(x_parts, axis=0)
                else:
                    if n == 4:
                        parts = [bcast_row(h, 8 * i + 3, 8) for i in range(CHUNK // 8)]
                    else:
                        parts = [jnp.where(sub < 4, bcast_row(h, 8 * i + 1, 8), bcast_row(h, 8 * i + 5, 8))
                                 for i in range(CHUNK // 8)]
                    d = neg_abs(b[h] - jnp.concatenate(parts, axis=0))
                    xs = jnp.where((row & n) != 0, q[h], k[h])
                x = (xs * jnp.exp2(d)).astype(BF16)
                scores[h] = jnp.where(ids == lid, _nt_dot(x, x), scores[h])

        o = []
        for h in heads:
            b_end = bcast_row(h, CHUNK - 1, CHUNK)
            st = st_ref[h % nh]
            oh = jnp.dot(scores[h].astype(BF16), v[h], preferred_element_type=F32)
            o.append(oh + _nt_dot((q[h] * jnp.exp2(b[h])).astype(BF16), st.astype(BF16)))
            k_dec = (k[h] * jnp.exp2(b_end - b[h])).astype(BF16)
            st_ref[h % nh] = st * jnp.exp2(b_end[:1]) + _tn_dot(v[h], k_dec)

        for h in heads:
            y = o[h] * _rms_scale(o[h]) * gn_ref[:, lanes[h]] * og_ref[rows[h], lanes[h]]
            o_ref[rows[h], lanes[h]] = y.astype(o_ref.dtype)
        return carry

    lax.fori_loop(0, nchunk // HGRN_CHUNKS_PER_ITER, chunk_body, 0)


def _hgrn(act, gnorm, d):
    t = act.shape[0]
    width = d // 2
    nh = min(HGRN_HEADS_PER_STEP, width // HEAD_DIM)
    bw = nh * HEAD_DIM
    ngroups = width // bw
    tr = _tile(t, 512, CHUNK)

    def seg_spec(seg):
        return pl.BlockSpec((tr, bw), lambda g, r, seg=seg: (r, seg * ngroups + g))

    return pl.pallas_call(
        functools.partial(_hgrn_kernel, nh=nh, nchunk=tr // CHUNK),
        out_shape=jax.ShapeDtypeStruct((t, width), BF16),
        grid=(ngroups, t // tr),
        in_specs=[seg_spec(1), seg_spec(2), seg_spec(3), seg_spec(4),
                  pl.BlockSpec((1, bw), lambda g, r: (0, g)),
                  pl.BlockSpec((CHUNK, CHUNK), lambda g, r: (0, 0))],
        out_specs=pl.BlockSpec((tr, bw), lambda g, r: (r, g)),
        scratch_shapes=[pltpu.VMEM((nh, HEAD_DIM, HEAD_DIM), F32),
                        pltpu.VMEM((nh * HGRN_CHUNKS_PER_ITER, CHUNK, HEAD_DIM), F32)],
        compiler_params=_params(("parallel", "arbitrary")),
        name="hgrn2",
    )(act, act, act, act, gnorm.reshape(1, width), jnp.asarray(_level_ids()))


def _merge_kernel(yp_ref, yh_ref, wp_ref, wh_ref, ga_ref, gb_ref, o_ref):
    a = jnp.dot(yp_ref[...], wp_ref[...].astype(BF16), preferred_element_type=F32)
    b = jnp.dot(yh_ref[...], wh_ref[...].astype(BF16), preferred_element_type=F32)
    o_ref[...] = (ga_ref[...] * a + gb_ref[...] * b).astype(o_ref.dtype)


def _merge(y_pool, y_hgrn, w_up_pool, w_up_hgrn, act, d):
    t, kp = y_pool.shape
    kh = y_hgrn.shape[1]
    tm = _tile(t, 2048, 8)
    tn = _tile(d // 2, 256)
    nb = d // tn
    off_a = (act.shape[1] - 2 * d) // tn
    return pl.pallas_call(
        _merge_kernel,
        out_shape=jax.ShapeDtypeStruct((t, d), BF16),
        grid=(t // tm, nb),
        in_specs=[pl.BlockSpec((tm, kp), lambda i, j: (i, 0)),
                  pl.BlockSpec((tm, kh), lambda i, j: (i, 0)),
                  pl.BlockSpec((kp, tn), lambda i, j: (0, j)),
                  pl.BlockSpec((kh, tn), lambda i, j: (0, j)),
                  pl.BlockSpec((tm, tn), lambda i, j: (i, off_a + j)),
                  pl.BlockSpec((tm, tn), lambda i, j: (i, off_a + nb + j))],
        out_specs=pl.BlockSpec((tm, tn), lambda i, j: (i, j)),
        compiler_params=_params(("parallel", "arbitrary")),
        name="gated_merge",
    )(y_pool, y_hgrn, w_up_pool, w_up_hgrn, act, act)


def _outproj_kernel(m_ref, w_ref, x_ref, h_ref):
    h_ref[...] = x_ref[...] + jnp.dot(m_ref[...], w_ref[...].astype(BF16), preferred_element_type=F32)


def _outproj(m, w, x):
    t, d = x.shape
    tm = _tile(t, 2048, 8)
    tn = _tile(d, 256)
    return pl.pallas_call(
        _outproj_kernel,
        out_shape=jax.ShapeDtypeStruct((t, d), F32),
        grid=(t // tm, d // tn),
        in_specs=[pl.BlockSpec((tm, d), lambda i, j: (i, 0)),
                  pl.BlockSpec((d, tn), lambda i, j: (0, j)),
                  pl.BlockSpec((tm, tn), lambda i, j: (i, j))],
        out_specs=pl.BlockSpec((tm, tn), lambda i, j: (i, j)),
        compiler_params=_params(("parallel", "arbitrary")),
        name="out_proj",
    )(m, w, x)


def _ffn_kernel(h_hbm, wg_ref, wu_ref, wd_ref, gin_ref, gout_ref, o_hbm, acc_ref, v_ref, sem,
                *, nj, tm, final_norm):
    i = pl.program_id(0)
    j = pl.program_id(1)
    chunk = min(tm, NORM_ROWS)
    starts = range(0, tm, chunk)

    def copy(c, r, to_vmem):
        hbm = (h_hbm if to_vmem else o_hbm).at[pl.ds(pl.multiple_of(i * tm, tm) + r, chunk), :]
        vmem = acc_ref.at[r:r + chunk, :]
        return pltpu.make_async_copy(hbm, vmem, sem.at[c]) if to_vmem else \
            pltpu.make_async_copy(vmem, hbm, sem.at[c])

    @pl.when(j == 0)
    def _():
        for c, r in enumerate(starts):
            copy(c, r, True).start()
        for c, r in enumerate(starts):
            copy(c, r, True).wait()
            scale = _rms_scale_ref(acc_ref, r, chunk)
            v_ref[r:r + chunk, :] = (acc_ref[r:r + chunk, :] * scale * gin_ref[...]).astype(BF16)

    v = v_ref[...]
    gate = jnp.dot(v, wg_ref[...].astype(BF16), preferred_element_type=F32)
    up = jnp.dot(v, wu_ref[...].astype(BF16), preferred_element_type=F32)
    a = (gate * _sigmoid(gate) * up).astype(BF16)
    acc_ref[...] += jnp.dot(a, wd_ref[...].astype(BF16), preferred_element_type=F32)

    @pl.when(j == nj - 1)
    def _():
        for c, r in enumerate(starts):
            if final_norm:
                scale = _rms_scale_ref(acc_ref, r, chunk)
                acc_ref[r:r + chunk, :] = acc_ref[r:r + chunk, :] * scale * gout_ref[...]
            copy(c, r, False).start()
        for c, r in enumerate(starts):
            copy(c, r, False).wait()


def _ffn(h, wg, wu, wd, g_in, g_out, final_norm):
    t, d = h.shape
    ff = wd.shape[0]
    tm = _tile(t, 1024, 8)
    tf = _tile(ff, 256)
    nj = ff // tf
    return pl.pallas_call(
        functools.partial(_ffn_kernel, nj=nj, tm=tm, final_norm=final_norm),
        out_shape=jax.ShapeDtypeStruct((t, d), F32),
        grid=(t // tm, nj),
        in_specs=[pl.BlockSpec(memory_space=pl.ANY),
                  pl.BlockSpec((d, tf), lambda i, j: (0, j)),
                  pl.BlockSpec((d, tf), lambda i, j: (0, j)),
                  pl.BlockSpec((tf, d), lambda i, j: (j, 0)),
                  pl.BlockSpec((1, d), lambda i, j: (0, 0)),
                  pl.BlockSpec((1, d), lambda i, j: (0, 0))],
        out_specs=pl.BlockSpec(memory_space=pl.ANY),
        scratch_shapes=[pltpu.VMEM((tm, d), F32), pltpu.VMEM((tm, d), BF16),
                        pltpu.SemaphoreType.DMA((pl.cdiv(tm, min(tm, NORM_ROWS)),))],
        compiler_params=_params(("arbitrary", "arbitrary")),
        name="swiglu_ffn",
    )(h, wg, wu, wd, g_in.reshape(1, d), g_out.reshape(1, d))


def kernel(x, g_mix, w_in, w_pool_group, pool_scale, lb_param, hgrn_norm, w_up_pool, w_up_hgrn, w_out,
           g_ffn, w_ffn_gate, w_ffn_up, w_ffn_down, g_final):
    bsz, t, d = x.shape
    depth = g_mix.shape[0]
    assert d % (2 * HEAD_DIM) == 0 and t % CHUNK == 0
    outs = []
    for bi in range(bsz):
        h = x[bi]
        for l in range(depth):
            act = _proj(h, g_mix[l], w_in[l], lb_param, l)
            y_pool = _pool(act, w_pool_group[l], pool_scale[l])
            y_hgrn = _hgrn(act, hgrn_norm[l], d)
            merged = _merge(y_pool, y_hgrn, w_up_pool[l], w_up_hgrn[l], act, d)
            h1 = _outproj(merged, w_out[l], h)
            h = _ffn(h1, w_ffn_gate[l], w_ffn_up[l], w_ffn_down[l], g_ffn[l], g_final, l == depth - 1)
        outs.append(h)
    return jnp.stack(outs, axis=0)
```

```python
import functools

import numpy as np
import jax
import jax.numpy as jnp
from jax import lax
from jax.experimental import pallas as pl
from jax.experimental.pallas import tpu as pltpu

EPS = 1e-6
POOL_WINDOWS = (2, 4, 8, 16)
POOL_HALO = 16
HEAD_DIM = 128
CHUNK = 64
NORM_ROWS = 256
MATMUL_SLAB_ROWS = 128
HGRN_HEADS_PER_STEP = 16
HGRN_CHUNKS_PER_ITER = 2
LEVELS = (32, 16, 8, 4, 2)

LOG2_E = 1.4426950408889634

F32 = jnp.float32
BF16 = jnp.bfloat16

VMEM_LIMIT_BYTES = 60 * 1024 * 1024


def _tile(dim, pref, align=128):
    if dim <= pref:
        return dim
    t = (pref // align) * align
    while t >= align:
        if dim % t == 0:
            return t
        t -= align
    return dim


def _params(sem):
    return pltpu.CompilerParams(dimension_semantics=sem, vmem_limit_bytes=VMEM_LIMIT_BYTES)


def _slabs(rows):
    n = min(rows, MATMUL_SLAB_ROWS)
    assert rows % n == 0
    return [slice(r, r + n) for r in range(0, rows, n)]


def _sigmoid(x):
    return 1.0 / (1.0 + jnp.exp(-x))


def _rms_scale(h):
    return lax.rsqrt(jnp.mean(h * h, axis=-1, keepdims=True) + EPS)


def _rms_scale_ref(ref, r, n):
    half = ref.shape[1] // 2
    lo = ref[r:r + n, :half]
    hi = ref[r:r + n, half:]
    ss = jnp.sum(lo * lo, axis=-1, keepdims=True) + jnp.sum(hi * hi, axis=-1, keepdims=True)
    return lax.rsqrt(ss / ref.shape[1] + EPS)


def _proj_kernel(x_hbm, g_ref, w_ref, lbp_ref, o_ref, u_ref, xbuf, sem, *, layer, tn, half):
    i = pl.program_id(0)
    tm = u_ref.shape[0]
    rc = xbuf.shape[1]

    def xcopy(c):
        src = x_hbm.at[pl.ds(pl.multiple_of(i * tm, tm) + c * rc, rc), :]
        return pltpu.make_async_copy(src, xbuf.at[c % 2], sem.at[c % 2])

    @pl.when(pl.program_id(1) == 0)
    def _():
        nchunks = tm // rc
        for c in range(min(2, nchunks)):
            xcopy(c).start()
        for c in range(nchunks):
            xcopy(c).wait()
            slot = xbuf.at[c % 2]
            scale = _rms_scale_ref(slot, 0, rc)
            u_ref[c * rc:(c + 1) * rc, :] = (slot[...] * scale * g_ref[...]).astype(BF16)
            if c + 2 < nchunks:
                xcopy(c + 2).start()

    seg = (pl.program_id(1) * tn) // half
    p = lbp_ref[...]
    e = jnp.exp(p - jnp.max(p, axis=0, keepdims=True))
    lb = jnp.sum(e[: layer + 1], axis=0, keepdims=True) / jnp.sum(e, axis=0, keepdims=True)
    is_f = seg == 2
    is_silu = jnp.logical_or(seg == 1, seg == 4)
    c0 = jnp.where(is_f, lb, 0.0)
    c1 = jnp.where(jnp.logical_or(seg == 0, seg == 3), 1.0, 0.0)
    c2 = jnp.where(is_f, 1.0 - lb, jnp.where(seg >= 5, 1.0, 0.0))
    c3 = jnp.where(is_silu, 1.0, 0.0)
    w = w_ref[...].astype(BF16)
    for rows in _slabs(tm):
        acc = jnp.dot(u_ref[rows, :], w, preferred_element_type=F32)
        o_ref[rows, :] = c0 + c1 * acc + _sigmoid(acc) * (c2 + c3 * acc)


def _proj(x, g, w, lb_param, layer):
    t, d = x.shape
    n_in = w.shape[1]
    half = d // 2
    tm = _tile(t, 2048, 8)
    tn = _tile(half, 512)
    rc = _tile(tm, NORM_ROWS, 8)
    nb = half // tn
    return pl.pallas_call(
        functools.partial(_proj_kernel, layer=layer, tn=tn, half=half),
        out_shape=jax.ShapeDtypeStruct((t, n_in), F32),
        grid=(t // tm, n_in // tn),
        in_specs=[pl.BlockSpec(memory_space=pl.ANY),
                  pl.BlockSpec((1, d), lambda i, j: (0, 0)),
                  pl.BlockSpec((d, tn), lambda i, j: (0, j)),
                  pl.BlockSpec((lb_param.shape[0], tn), lambda i, j: (0, jnp.clip(j - 2 * nb, 0, nb - 1)))],
        out_specs=pl.BlockSpec((tm, tn), lambda i, j: (i, j)),
        scratch_shapes=[pltpu.VMEM((tm, d), BF16), pltpu.VMEM((2, rc, d), F32), pltpu.SemaphoreType.DMA((2,))],
        compiler_params=_params(("arbitrary", "arbitrary")),
        name="in_proj",
    )(x, g.reshape(1, d), w, lb_param)


def _pool_kernel(z_ref, halo_ref, w_ref, s_ref, o_ref, *, tp):
    i = pl.program_id(0)
    g = pl.program_id(1)
    z = z_ref[...]
    halo = jnp.where(i > 0, halo_ref[...], 0.0)
    full = jnp.concatenate([halo, z], axis=0)
    t = i * tp + lax.broadcasted_iota(jnp.int32, (tp, 1), 0)

    for gi, w in enumerate(POOL_WINDOWS):
        @pl.when(g == gi)
        def _(w=w):
            s = full
            k = 1
            while k < w:
                s = s + pltpu.roll(s, k, 0)
                k *= 2
            count = jnp.minimum(t + 1, w).astype(F32)
            pooled = s[POOL_HALO:] / count - z
            y = jnp.dot(pooled.astype(BF16), w_ref[0].astype(BF16), preferred_element_type=F32)
            o_ref[...] = (y * s_ref[0]).astype(o_ref.dtype)


def _pool(act, w_group, scale):
    t = act.shape[0]
    ng, gsz, _ = w_group.shape
    assert ng == len(POOL_WINDOWS)
    tp = _tile(t, 2048, POOL_HALO)
    hb = tp // POOL_HALO
    return pl.pallas_call(
        functools.partial(_pool_kernel, tp=tp),
        out_shape=jax.ShapeDtypeStruct((t, ng * gsz), BF16),
        grid=(t // tp, ng),
        in_specs=[pl.BlockSpec((tp, gsz), lambda i, g: (i, g)),
                  pl.BlockSpec((POOL_HALO, gsz), lambda i, g: (jnp.maximum(i * hb - 1, 0), g)),
                  pl.BlockSpec((1, gsz, gsz), lambda i, g: (g, 0, 0)),
                  pl.BlockSpec((1, 1, gsz), lambda i, g: (g, 0, 0))],
        out_specs=pl.BlockSpec((tp, gsz), lambda i, g: (i, g)),
        compiler_params=_params(("parallel", "arbitrary")),
        name="pool_mixer",
    )(act, act, w_group, scale.reshape(ng, 1, gsz))


def _level_ids():
    t = np.arange(CHUNK)[:, None]
    s = np.arange(CHUNK)[None, :]
    ids = -np.ones((CHUNK, CHUNK), np.int32)
    ids[t == s] = 0
    for lid, n in enumerate((1,) + tuple(reversed(LEVELS)), start=1):
        m = (t // (2 * n) == s // (2 * n)) & (t % (2 * n) >= n) & (s % (2 * n) < n)
        ids[m] = lid
    return ids


def _nt_dot(a, b):
    return lax.dot_general(a, b, (((1,), (1,)), ((), ())), preferred_element_type=F32)


def _tn_dot(a, b):
    return lax.dot_general(a, b, (((0,), (0,)), ((), ())), preferred_element_type=F32)


def _hgrn_kernel(q_ref, f_ref, v_ref, og_ref, gn_ref, ids_ref, o_ref, st_ref, b_ref, *, nh, nchunk):
    @pl.when(pl.program_id(1) == 0)
    def _():
        st_ref[...] = jnp.zeros_like(st_ref)

    ids = ids_ref[...]
    sub = lax.broadcasted_iota(jnp.int32, (8, HEAD_DIM), 0)
    row = lax.broadcasted_iota(jnp.int32, (CHUNK, HEAD_DIM), 0)
    tril = (lax.broadcasted_iota(jnp.int32, (CHUNK, CHUNK), 0)
            >= lax.broadcasted_iota(jnp.int32, (CHUNK, CHUNK), 1)).astype(BF16)

    def bcast_row(h, r, n):
        return jnp.broadcast_to(b_ref[h, r:r + 1, :], (n, HEAD_DIM))

    def neg_abs(x):
        return pltpu.bitcast(pltpu.bitcast(x, jnp.uint32) | jnp.uint32(0x80000000), F32)

    def chunk_body(c, carry):
        heads = range(nh * HGRN_CHUNKS_PER_ITER)
        rows = [pl.ds(pl.multiple_of((c * HGRN_CHUNKS_PER_ITER + h // nh) * CHUNK, CHUNK), CHUNK) for h in heads]
        lanes = [slice((h % nh) * HEAD_DIM, (h % nh + 1) * HEAD_DIM) for h in heads]
        q = [q_ref[rows[h], lanes[h]] for h in heads]
        f = [f_ref[rows[h], lanes[h]] for h in heads]
        v = [v_ref[rows[h], lanes[h]].astype(BF16) for h in heads]
        k = [1.0 - f[h] for h in heads]

        b = []
        for h in heads:
            g = jnp.log(f[h]) * LOG2_E
            g_hi = g.astype(BF16)
            g_r = g - g_hi.astype(F32)
            g_mid = g_r.astype(BF16)
            g_lo = (g_r - g_mid.astype(F32)).astype(BF16)
            b.append(jnp.dot(tril, g_hi, preferred_element_type=F32)
                     + jnp.dot(tril, g_mid, preferred_element_type=F32)
                     + jnp.dot(tril, g_lo, preferred_element_type=F32))
        for h in heads:
            b_ref[h] = b[h]

        k_bf = [k[h].astype(BF16) for h in heads]
        scores = [jnp.where(ids == 0, _nt_dot(q[h].astype(BF16), k_bf[h]), 0.0) for h in heads]
        scores = [jnp.where(ids == 1, _nt_dot((q[h] * f[h]).astype(BF16), k_bf[h]), scores[h]) for h in heads]
        for lid, n in enumerate(reversed(LEVELS), start=2):
            for h in heads:
                if n >= 8:
                    d_parts, x_parts = [], []
                    for base in range(0, CHUNK, 2 * n):
                        lo, up = slice(base, base + n), slice(base + n, base + 2 * n)
                        mid = bcast_row(h, base + n - 1, n)
                        d_parts += [mid - b[h][lo], b[h][up] - mid]
                        x_parts += [k[h][lo], q[h][up]]
                    d = jnp.concatenate(d_parts, axis=0)
                    xs = jnp.concatenate(x_parts, axis=0)
                else:
                    if n == 4:
                        parts = [bcast_row(h, 8 * i + 3, 8) for i in range(CHUNK // 8)]
                    else:
                        parts = [jnp.where(sub < 4, bcast_row(h, 8 * i + 1, 8), bcast_row(h, 8 * i + 5, 8))
                                 for i in range(CHUNK // 8)]
                    d = neg_abs(b[h] - jnp.concatenate(parts, axis=0))
                    xs = jnp.where((row & n) != 0, q[h], k[h])
                x = (xs * jnp.exp2(d)).astype(BF16)
                scores[h] = jnp.where(ids == lid, _nt_dot(x, x), scores[h])

        o = []
        for h in heads:
            b_end = bcast_row(h, CHUNK - 1, CHUNK)
            st = st_ref[h % nh]
            oh = jnp.dot(scores[h].astype(BF16), v[h], preferred_element_type=F32)
            o.append(oh + _nt_dot((q[h] * jnp.exp2(b[h])).astype(BF16), st.astype(BF16)))
            k_dec = (k[h] * jnp.exp2(b_end - b[h])).astype(BF16)
            st_ref[h % nh] = st * jnp.exp2(b_end[:1]) + _tn_dot(v[h], k_dec)

        for h in heads:
            y = o[h] * _rms_scale(o[h]) * gn_ref[:, lanes[h]] * og_ref[rows[h], lanes[h]]
            o_ref[rows[h], lanes[h]] = y.astype(o_ref.dtype)
        return carry

    lax.fori_loop(0, nchunk // HGRN_CHUNKS_PER_ITER, chunk_body, 0)


def _hgrn(act, gnorm, d):
    t = act.shape[0]
    width = d // 2
    nh = min(HGRN_HEADS_PER_STEP, width // HEAD_DIM)
    bw = nh * HEAD_DIM
    ngroups = width // bw
    tr = _tile(t, 512, CHUNK)

    def seg_spec(seg):
        return pl.BlockSpec((tr, bw), lambda g, r, seg=seg: (r, seg * ngroups + g))

    return pl.pallas_call(
        functools.partial(_hgrn_kernel, nh=nh, nchunk=tr // CHUNK),
        out_shape=jax.ShapeDtypeStruct((t, width), BF16),
        grid=(ngroups, t // tr),
        in_specs=[seg_spec(1), seg_spec(2), seg_spec(3), seg_spec(4),
                  pl.BlockSpec((1, bw), lambda g, r: (0, g)),
                  pl.BlockSpec((CHUNK, CHUNK), lambda g, r: (0, 0))],
        out_specs=pl.BlockSpec((tr, bw), lambda g, r: (r, g)),
        scratch_shapes=[pltpu.VMEM((nh, HEAD_DIM, HEAD_DIM), F32),
                        pltpu.VMEM((nh * HGRN_CHUNKS_PER_ITER, CHUNK, HEAD_DIM), F32)],
        compiler_params=_params(("parallel", "arbitrary")),
        name="hgrn2",
    )(act, act, act, act, gnorm.reshape(1, width), jnp.asarray(_level_ids()))


def _merge_kernel(yp_ref, yh_ref, wp_ref, wh_ref, ga_ref, gb_ref, o_ref):
    a = jnp.dot(yp_ref[...], wp_ref[...].astype(BF16), preferred_element_type=F32)
    b = jnp.dot(yh_ref[...], wh_ref[...].astype(BF16), preferred_element_type=F32)
    o_ref[...] = (ga_ref[...] * a + gb_ref[...] * b).astype(o_ref.dtype)


def _merge(y_pool, y_hgrn, w_up_pool, w_up_hgrn, act, d):
    t, kp = y_pool.shape
    kh = y_hgrn.shape[1]
    tm = _tile(t, 2048, 8)
    tn = _tile(d // 2, 256)
    nb = d // tn
    off_a = (act.shape[1] - 2 * d) // tn
    return pl.pallas_call(
        _merge_kernel,
        out_shape=jax.ShapeDtypeStruct((t, d), BF16),
        grid=(t // tm, nb),
        in_specs=[pl.BlockSpec((tm, kp), lambda i, j: (i, 0)),
                  pl.BlockSpec((tm, kh), lambda i, j: (i, 0)),
                  pl.BlockSpec((kp, tn), lambda i, j: (0, j)),
                  pl.BlockSpec((kh, tn), lambda i, j: (0, j)),
                  pl.BlockSpec((tm, tn), lambda i, j: (i, off_a + j)),
                  pl.BlockSpec((tm, tn), lambda i, j: (i, off_a + nb + j))],
        out_specs=pl.BlockSpec((tm, tn), lambda i, j: (i, j)),
        compiler_params=_params(("parallel", "arbitrary")),
        name="gated_merge",
    )(y_pool, y_hgrn, w_up_pool, w_up_hgrn, act, act)


def _outproj_kernel(m_ref, w_ref, x_ref, h_ref):
    w = w_ref[...].astype(BF16)
    for rows in _slabs(h_ref.shape[0]):
        h_ref[rows, :] = x_ref[rows, :] + jnp.dot(m_ref[rows, :], w, preferred_element_type=F32)


def _outproj(m, w, x):
    t, d = x.shape
    tm = _tile(t, 2048, 8)
    tn = _tile(d, 256)
    return pl.pallas_call(
        _outproj_kernel,
        out_shape=jax.ShapeDtypeStruct((t, d), F32),
        grid=(t // tm, d // tn),
        in_specs=[pl.BlockSpec((tm, d), lambda i, j: (i, 0)),
                  pl.BlockSpec((d, tn), lambda i, j: (0, j)),
                  pl.BlockSpec((tm, tn), lambda i, j: (i, j))],
        out_specs=pl.BlockSpec((tm, tn), lambda i, j: (i, j)),
        compiler_params=_params(("parallel", "arbitrary")),
        name="out_proj",
    )(m, w, x)


def _ffn_kernel(h_hbm, wg_ref, wu_ref, wd_ref, gin_ref, gout_ref, o_hbm, acc_ref, v_ref, sem,
                *, nj, tm, final_norm):
    i = pl.program_id(0)
    j = pl.program_id(1)
    chunk = min(tm, NORM_ROWS)
    starts = range(0, tm, chunk)

    def copy(c, r, to_vmem):
        hbm = (h_hbm if to_vmem else o_hbm).at[pl.ds(pl.multiple_of(i * tm, tm) + r, chunk), :]
        vmem = acc_ref.at[r:r + chunk, :]
        return pltpu.make_async_copy(hbm, vmem, sem.at[c]) if to_vmem else \
            pltpu.make_async_copy(vmem, hbm, sem.at[c])

    @pl.when(j == 0)
    def _():
        for c, r in enumerate(starts):
            copy(c, r, True).start()
        for c, r in enumerate(starts):
            copy(c, r, True).wait()
            scale = _rms_scale_ref(acc_ref, r, chunk)
            v_ref[r:r + chunk, :] = (acc_ref[r:r + chunk, :] * scale * gin_ref[...]).astype(BF16)

    v = v_ref[...]
    gate = jnp.dot(v, wg_ref[...].astype(BF16), preferred_element_type=F32)
    up = jnp.dot(v, wu_ref[...].astype(BF16), preferred_element_type=F32)
    a = (gate * _sigmoid(gate) * up).astype(BF16)
    acc_ref[...] += jnp.dot(a, wd_ref[...].astype(BF16), preferred_element_type=F32)

    @pl.when(j == nj - 1)
    def _():
        for c, r in enumerate(starts):
            if final_norm:
                scale = _rms_scale_ref(acc_ref, r, chunk)
                acc_ref[r:r + chunk, :] = acc_ref[r:r + chunk, :] * scale * gout_ref[...]
            copy(c, r, False).start()
        for c, r in enumerate(starts):
            copy(c, r, False).wait()


def _ffn(h, wg, wu, wd, g_in, g_out, final_norm):
    t, d = h.shape
    ff = wd.shape[0]
    tm = _tile(t, 1024, 8)
    tf = _tile(ff, 256)
    nj = ff // tf
    return pl.pallas_call(
        functools.partial(_ffn_kernel, nj=nj, tm=tm, final_norm=final_norm),
        out_shape=jax.ShapeDtypeStruct((t, d), F32),
        grid=(t // tm, nj),
        in_specs=[pl.BlockSpec(memory_space=pl.ANY),
                  pl.BlockSpec((d, tf), lambda i, j: (0, j)),
                  pl.BlockSpec((d, tf), lambda i, j: (0, j)),
                  pl.BlockSpec((tf, d), lambda i, j: (j, 0)),
                  pl.BlockSpec((1, d), lambda i, j: (0, 0)),
                  pl.BlockSpec((1, d), lambda i, j: (0, 0))],
        out_specs=pl.BlockSpec(memory_space=pl.ANY),
        scratch_shapes=[pltpu.VMEM((tm, d), F32), pltpu.VMEM((tm, d), BF16),
                        pltpu.SemaphoreType.DMA((pl.cdiv(tm, min(tm, NORM_ROWS)),))],
        compiler_params=_params(("arbitrary", "arbitrary")),
        name="swiglu_ffn",
    )(h, wg, wu, wd, g_in.reshape(1, d), g_out.reshape(1, d))


def kernel(x, g_mix, w_in, w_pool_group, pool_scale, lb_param, hgrn_norm, w_up_pool, w_up_hgrn, w_out,
           g_ffn, w_ffn_gate, w_ffn_up, w_ffn_down, g_final):
    bsz, t, d = x.shape
    depth = g_mix.shape[0]
    assert d % (2 * HEAD_DIM) == 0 and t % CHUNK == 0
    outs = []
    for bi in range(bsz):
        h = x[bi]
        for l in range(depth):
            act = _proj(h, g_mix[l], w_in[l], lb_param, l)
            y_pool = _pool(act, w_pool_group[l], pool_scale[l])
            y_hgrn = _hgrn(act, hgrn_norm[l], d)
            merged = _merge(y_pool, y_hgrn, w_up_pool[l], w_up_hgrn[l], act, d)
            h1 = _outproj(merged, w_out[l], h)
            h = _ffn(h1, w_ffn_gate[l], w_ffn_up[l], w_ffn_down[l], g_ffn[l], g_final, l == depth - 1)
        outs.append(h)
    return jnp.stack(outs, axis=0)
```

```python
import functools

import numpy as np
import jax
import jax.numpy as jnp
from jax import lax
from jax.experimental import pallas as pl
from jax.experimental.pallas import tpu as pltpu

EPS = 1e-6
POOL_WINDOWS = (2, 4, 8, 16)
POOL_HALO = 16
HEAD_DIM = 128
CHUNK = 64
NORM_ROWS = 256
PROJ_SLAB_ROWS = 512
OUTPROJ_SLAB_ROWS = 128
HGRN_HEADS_PER_STEP = 16
HGRN_CHUNKS_PER_ITER = 2
LEVELS = (32, 16, 8, 4, 2)

LOG2_E = 1.4426950408889634

F32 = jnp.float32
BF16 = jnp.bfloat16

VMEM_LIMIT_BYTES = 60 * 1024 * 1024


def _tile(dim, pref, align=128):
    if dim <= pref:
        return dim
    t = (pref // align) * align
    while t >= align:
        if dim % t == 0:
            return t
        t -= align
    return dim


def _params(sem):
    return pltpu.CompilerParams(dimension_semantics=sem, vmem_limit_bytes=VMEM_LIMIT_BYTES)


def _slabs(rows, slab):
    n = min(rows, slab)
    assert rows % n == 0
    return [slice(r, r + n) for r in range(0, rows, n)]


def _sigmoid(x):
    return 1.0 / (1.0 + jnp.exp(-x))


def _rms_scale(h):
    return lax.rsqrt(jnp.mean(h * h, axis=-1, keepdims=True) + EPS)


def _rms_scale_ref(ref, r, n):
    half = ref.shape[1] // 2
    lo = ref[r:r + n, :half]
    hi = ref[r:r + n, half:]
    ss = jnp.sum(lo * lo, axis=-1, keepdims=True) + jnp.sum(hi * hi, axis=-1, keepdims=True)
    return lax.rsqrt(ss / ref.shape[1] + EPS)


def _proj_kernel(x_hbm, g_ref, w_ref, lbp_ref, o_ref, u_ref, xbuf, sem, *, layer, tn, half):
    i = pl.program_id(0)
    tm = u_ref.shape[0]
    rc = xbuf.shape[1]

    def xcopy(c):
        src = x_hbm.at[pl.ds(pl.multiple_of(i * tm, tm) + c * rc, rc), :]
        return pltpu.make_async_copy(src, xbuf.at[c % 2], sem.at[c % 2])

    @pl.when(pl.program_id(1) == 0)
    def _():
        nchunks = tm // rc
        for c in range(min(2, nchunks)):
            xcopy(c).start()
        for c in range(nchunks):
            xcopy(c).wait()
            slot = xbuf.at[c % 2]
            scale = _rms_scale_ref(slot, 0, rc)
            u_ref[c * rc:(c + 1) * rc, :] = (slot[...] * scale * g_ref[...]).astype(BF16)
            if c + 2 < nchunks:
                xcopy(c + 2).start()

    seg = (pl.program_id(1) * tn) // half
    p = lbp_ref[...]
    e = jnp.exp(p - jnp.max(p, axis=0, keepdims=True))
    lb = jnp.sum(e[: layer + 1], axis=0, keepdims=True) / jnp.sum(e, axis=0, keepdims=True)
    is_f = seg == 2
    is_silu = jnp.logical_or(seg == 1, seg == 4)
    c0 = jnp.where(is_f, lb, 0.0)
    c1 = jnp.where(jnp.logical_or(seg == 0, seg == 3), 1.0, 0.0)
    c2 = jnp.where(is_f, 1.0 - lb, jnp.where(seg >= 5, 1.0, 0.0))
    c3 = jnp.where(is_silu, 1.0, 0.0)
    w = w_ref[...].astype(BF16)
    for rows in _slabs(tm, PROJ_SLAB_ROWS):
        acc = jnp.dot(u_ref[rows, :], w, preferred_element_type=F32)
        o_ref[rows, :] = c0 + c1 * acc + _sigmoid(acc) * (c2 + c3 * acc)


def _proj(x, g, w, lb_param, layer):
    t, d = x.shape
    n_in = w.shape[1]
    half = d // 2
    tm = _tile(t, 2048, 8)
    tn = _tile(half, 512)
    rc = _tile(tm, NORM_ROWS, 8)
    nb = half // tn
    return pl.pallas_call(
        functools.partial(_proj_kernel, layer=layer, tn=tn, half=half),
        out_shape=jax.ShapeDtypeStruct((t, n_in), F32),
        grid=(t // tm, n_in // tn),
        in_specs=[pl.BlockSpec(memory_space=pl.ANY),
                  pl.BlockSpec((1, d), lambda i, j: (0, 0)),
                  pl.BlockSpec((d, tn), lambda i, j: (0, j)),
                  pl.BlockSpec((lb_param.shape[0], tn), lambda i, j: (0, jnp.clip(j - 2 * nb, 0, nb - 1)))],
        out_specs=pl.BlockSpec((tm, tn), lambda i, j: (i, j)),
        scratch_shapes=[pltpu.VMEM((tm, d), BF16), pltpu.VMEM((2, rc, d), F32), pltpu.SemaphoreType.DMA((2,))],
        compiler_params=_params(("arbitrary", "arbitrary")),
        name="in_proj",
    )(x, g.reshape(1, d), w, lb_param)


def _pool_kernel(z_ref, halo_ref, w_ref, s_ref, o_ref, *, tp):
    i = pl.program_id(0)
    g = pl.program_id(1)
    z = z_ref[...]
    halo = jnp.where(i > 0, halo_ref[...], 0.0)
    full = jnp.concatenate([halo, z], axis=0)
    t = i * tp + lax.broadcasted_iota(jnp.int32, (tp, 1), 0)

    for gi, w in enumerate(POOL_WINDOWS):
        @pl.when(g == gi)
        def _(w=w):
            s = full
            k = 1
            while k < w:
                s = s + pltpu.roll(s, k, 0)
                k *= 2
            count = jnp.minimum(t + 1, w).astype(F32)
            pooled = s[POOL_HALO:] / count - z
            y = jnp.dot(pooled.astype(BF16), w_ref[0].astype(BF16), preferred_element_type=F32)
            o_ref[...] = (y * s_ref[0]).astype(o_ref.dtype)


def _pool(act, w_group, scale):
    t = act.shape[0]
    ng, gsz, _ = w_group.shape
    assert ng == len(POOL_WINDOWS)
    tp = _tile(t, 2048, POOL_HALO)
    hb = tp // POOL_HALO
    return pl.pallas_call(
        functools.partial(_pool_kernel, tp=tp),
        out_shape=jax.ShapeDtypeStruct((t, ng * gsz), BF16),
        grid=(t // tp, ng),
        in_specs=[pl.BlockSpec((tp, gsz), lambda i, g: (i, g)),
                  pl.BlockSpec((POOL_HALO, gsz), lambda i, g: (jnp.maximum(i * hb - 1, 0), g)),
                  pl.BlockSpec((1, gsz, gsz), lambda i, g: (g, 0, 0)),
                  pl.BlockSpec((1, 1, gsz), lambda i, g: (g, 0, 0))],
        out_specs=pl.BlockSpec((tp, gsz), lambda i, g: (i, g)),
        compiler_params=_params(("parallel", "arbitrary")),
        name="pool_mixer",
    )(act, act, w_group, scale.reshape(ng, 1, gsz))


def _level_ids():
    t = np.arange(CHUNK)[:, None]
    s = np.arange(CHUNK)[None, :]
    ids = -np.ones((CHUNK, CHUNK), np.int32)
    ids[t == s] = 0
    for lid, n in enumerate((1,) + tuple(reversed(LEVELS)), start=1):
        m = (t // (2 * n) == s // (2 * n)) & (t % (2 * n) >= n) & (s % (2 * n) < n)
        ids[m] = lid
    return ids


def _nt_dot(a, b):
    return lax.dot_general(a, b, (((1,), (1,)), ((), ())), preferred_element_type=F32)


def _tn_dot(a, b):
    return lax.dot_general(a, b, (((0,), (0,)), ((), ())), preferred_element_type=F32)


def _hgrn_kernel(q_ref, f_ref, v_ref, og_ref, gn_ref, ids_ref, o_ref, st_ref, b_ref, *, nh, nchunk):
    @pl.when(pl.program_id(1) == 0)
    def _():
        st_ref[...] = jnp.zeros_like(st_ref)

    ids = ids_ref[...]
    sub = lax.broadcasted_iota(jnp.int32, (8, HEAD_DIM), 0)
    row = lax.broadcasted_iota(jnp.int32, (CHUNK, HEAD_DIM), 0)
    tril = (lax.broadcasted_iota(jnp.int32, (CHUNK, CHUNK), 0)
            >= lax.broadcasted_iota(jnp.int32, (CHUNK, CHUNK), 1)).astype(BF16)

    def bcast_row(h, r, n):
        return jnp.broadcast_to(b_ref[h, r:r + 1, :], (n, HEAD_DIM))

    def neg_abs(x):
        return pltpu.bitcast(pltpu.bitcast(x, jnp.uint32) | jnp.uint32(0x80000000), F32)

    def chunk_body(c, carry):
        heads = range(nh * HGRN_CHUNKS_PER_ITER)
        rows = [pl.ds(pl.multiple_of((c * HGRN_CHUNKS_PER_ITER + h // nh) * CHUNK, CHUNK), CHUNK) for h in heads]
        lanes = [slice((h % nh) * HEAD_DIM, (h % nh + 1) * HEAD_DIM) for h in heads]
        q = [q_ref[rows[h], lanes[h]] for h in heads]
        f = [f_ref[rows[h], lanes[h]] for h in heads]
        v = [v_ref[rows[h], lanes[h]].astype(BF16) for h in heads]
        k = [1.0 - f[h] for h in heads]

        b = []
        for h in heads:
            g = jnp.log(f[h]) * LOG2_E
            g_hi = g.astype(BF16)
            g_r = g - g_hi.astype(F32)
            g_mid = g_r.astype(BF16)
            g_lo = (g_r - g_mid.astype(F32)).astype(BF16)
            b.append(jnp.dot(tril, g_hi, preferred_element_type=F32)
                     + jnp.dot(tril, g_mid, preferred_element_type=F32)
                     + jnp.dot(tril, g_lo, preferred_element_type=F32))
        for h in heads:
            b_ref[h] = b[h]

        k_bf = [k[h].astype(BF16) for h in heads]
        scores = [jnp.where(ids == 0, _nt_dot(q[h].astype(BF16), k_bf[h]), 0.0) for h in heads]
        scores = [jnp.where(ids == 1, _nt_dot((q[h] * f[h]).astype(BF16), k_bf[h]), scores[h]) for h in heads]
        for lid, n in enumerate(reversed(LEVELS), start=2):
            for h in heads:
                if n >= 8:
                    d_parts, x_parts = [], []
                    for base in range(0, CHUNK, 2 * n):
                        lo, up = slice(base, base + n), slice(base + n, base + 2 * n)
                        mid = bcast_row(h, base + n - 1, n)
                        d_parts += [mid - b[h][lo], b[h][up] - mid]
                        x_parts += [k[h][lo], q[h][up]]
                    d = jnp.concatenate(d_parts, axis=0)
                    xs = jnp.concatenate(x_parts, axis=0)
                else:
                    if n == 4:
                        parts = [bcast_row(h, 8 * i + 3, 8) for i in range(CHUNK // 8)]
                    else:
                        parts = [jnp.where(sub < 4, bcast_row(h, 8 * i + 1, 8), bcast_row(h, 8 * i + 5, 8))
                                 for i in range(CHUNK // 8)]
                    d = neg_abs(b[h] - jnp.concatenate(parts, axis=0))
                    xs = jnp.where((row & n) != 0, q[h], k[h])
                x = (xs * jnp.exp2(d)).astype(BF16)
                scores[h] = jnp.where(ids == lid, _nt_dot(x, x), scores[h])

        o = []
        for h in heads:
            b_end = bcast_row(h, CHUNK - 1, CHUNK)
            st = st_ref[h % nh]
            oh = jnp.dot(scores[h].astype(BF16), v[h], preferred_element_type=F32)
            o.append(oh + _nt_dot((q[h] * jnp.exp2(b[h])).astype(BF16), st.astype(BF16)))
            k_dec = (k[h] * jnp.exp2(b_end - b[h])).astype(BF16)
            st_ref[h % nh] = st * jnp.exp2(b_end[:1]) + _tn_dot(v[h], k_dec)

        for h in heads:
            y = o[h] * _rms_scale(o[h]) * gn_ref[:, lanes[h]] * og_ref[rows[h], lanes[h]]
            o_ref[rows[h], lanes[h]] = y.astype(o_ref.dtype)
        return carry

    lax.fori_loop(0, nchunk // HGRN_CHUNKS_PER_ITER, chunk_body, 0)


def _hgrn(act, gnorm, d):
    t = act.shape[0]
    width = d // 2
    nh = min(HGRN_HEADS_PER_STEP, width // HEAD_DIM)
    bw = nh * HEAD_DIM
    ngroups = width // bw
    tr = _tile(t, 512, CHUNK)

    def seg_spec(seg):
        return pl.BlockSpec((tr, bw), lambda g, r, seg=seg: (r, seg * ngroups + g))

    return pl.pallas_call(
        functools.partial(_hgrn_kernel, nh=nh, nchunk=tr // CHUNK),
        out_shape=jax.ShapeDtypeStruct((t, width), BF16),
        grid=(ngroups, t // tr),
        in_specs=[seg_spec(1), seg_spec(2), seg_spec(3), seg_spec(4),
                  pl.BlockSpec((1, bw), lambda g, r: (0, g)),
                  pl.BlockSpec((CHUNK, CHUNK), lambda g, r: (0, 0))],
        out_specs=pl.BlockSpec((tr, bw), lambda g, r: (r, g)),
        scratch_shapes=[pltpu.VMEM((nh, HEAD_DIM, HEAD_DIM), F32),
                        pltpu.VMEM((nh * HGRN_CHUNKS_PER_ITER, CHUNK, HEAD_DIM), F32)],
        compiler_params=_params(("parallel", "arbitrary")),
        name="hgrn2",
    )(act, act, act, act, gnorm.reshape(1, width), jnp.asarray(_level_ids()))


def _merge_kernel(yp_ref, yh_ref, wp_ref, wh_ref, ga_ref, gb_ref, o_ref):
    a = jnp.dot(yp_ref[...], wp_ref[...].astype(BF16), preferred_element_type=F32)
    b = jnp.dot(yh_ref[...], wh_ref[...].astype(BF16), preferred_element_type=F32)
    o_ref[...] = (ga_ref[...] * a + gb_ref[...] * b).astype(o_ref.dtype)


def _merge(y_pool, y_hgrn, w_up_pool, w_up_hgrn, act, d):
    t, kp = y_pool.shape
    kh = y_hgrn.shape[1]
    tm = _tile(t, 2048, 8)
    tn = _tile(d // 2, 256)
    nb = d // tn
    off_a = (act.shape[1] - 2 * d) // tn
    return pl.pallas_call(
        _merge_kernel,
        out_shape=jax.ShapeDtypeStruct((t, d), BF16),
        grid=(t // tm, nb),
        in_specs=[pl.BlockSpec((tm, kp), lambda i, j: (i, 0)),
                  pl.BlockSpec((tm, kh), lambda i, j: (i, 0)),
                  pl.BlockSpec((kp, tn), lambda i, j: (0, j)),
                  pl.BlockSpec((kh, tn), lambda i, j: (0, j)),
                  pl.BlockSpec((tm, tn), lambda i, j: (i, off_a + j)),
                  pl.BlockSpec((tm, tn), lambda i, j: (i, off_a + nb + j))],
        out_specs=pl.BlockSpec((tm, tn), lambda i, j: (i, j)),
        compiler_params=_params(("parallel", "arbitrary")),
        name="gated_merge",
    )(y_pool, y_hgrn, w_up_pool, w_up_hgrn, act, act)


def _outproj_kernel(m_ref, w_ref, x_ref, h_ref):
    w = w_ref[...].astype(BF16)
    for rows in _slabs(h_ref.shape[0], OUTPROJ_SLAB_ROWS):
        h_ref[rows, :] = x_ref[rows, :] + jnp.dot(m_ref[rows, :], w, preferred_element_type=F32)


def _outproj(m, w, x):
    t, d = x.shape
    tm = _tile(t, 2048, 8)
    tn = _tile(d, 256)
    return pl.pallas_call(
        _outproj_kernel,
        out_shape=jax.ShapeDtypeStruct((t, d), F32),
        grid=(t // tm, d // tn),
        in_specs=[pl.BlockSpec((tm, d), lambda i, j: (i, 0)),
                  pl.BlockSpec((d, tn), lambda i, j: (0, j)),
                  pl.BlockSpec((tm, tn), lambda i, j: (i, j))],
        out_specs=pl.BlockSpec((tm, tn), lambda i, j: (i, j)),
        compiler_params=_params(("parallel", "arbitrary")),
        name="out_proj",
    )(m, w, x)


def _ffn_kernel(h_hbm, wg_ref, wu_ref, wd_ref, gin_ref, gout_ref, o_hbm, acc_ref, v_ref, sem,
                *, nj, tm, final_norm):
    i = pl.program_id(0)
    j = pl.program_id(1)
    chunk = min(tm, NORM_ROWS)
    starts = range(0, tm, chunk)

    def copy(c, r, to_vmem):
        hbm = (h_hbm if to_vmem else o_hbm).at[pl.ds(pl.multiple_of(i * tm, tm) + r, chunk), :]
        vmem = acc_ref.at[r:r + chunk, :]
        return pltpu.make_async_copy(hbm, vmem, sem.at[c]) if to_vmem else \
            pltpu.make_async_copy(vmem, hbm, sem.at[c])

    @pl.when(j == 0)
    def _():
        for c, r in enumerate(starts):
            copy(c, r, True).start()
        for c, r in enumerate(starts):
            copy(c, r, True).wait()
            scale = _rms_scale_ref(acc_ref, r, chunk)
            v_ref[r:r + chunk, :] = (acc_ref[r:r + chunk, :] * scale * gin_ref[...]).astype(BF16)

    v = v_ref[...]
    gate = jnp.dot(v, wg_ref[...].astype(BF16), preferred_element_type=F32)
    up = jnp.dot(v, wu_ref[...].astype(BF16), preferred_element_type=F32)
    a = (gate * _sigmoid(gate) * up).astype(BF16)
    acc_ref[...] += jnp.dot(a, wd_ref[...].astype(BF16), preferred_element_type=F32)

    @pl.when(j == nj - 1)
    def _():
        for c, r in enumerate(starts):
            if final_norm:
                scale = _rms_scale_ref(acc_ref, r, chunk)
                acc_ref[r:r + chunk, :] = acc_ref[r:r + chunk, :] * scale * gout_ref[...]
            copy(c, r, False).start()
        for c, r in enumerate(starts):
            copy(c, r, False).wait()


def _ffn(h, wg, wu, wd, g_in, g_out, final_norm):
    t, d = h.shape
    ff = wd.shape[0]
    tm = _tile(t, 1024, 8)
    tf = _tile(ff, 256)
    nj = ff // tf
    return pl.pallas_call(
        functools.partial(_ffn_kernel, nj=nj, tm=tm, final_norm=final_norm),
        out_shape=jax.ShapeDtypeStruct((t, d), F32),
        grid=(t // tm, nj),
        in_specs=[pl.BlockSpec(memory_space=pl.ANY),
                  pl.BlockSpec((d, tf), lambda i, j: (0, j)),
                  pl.BlockSpec((d, tf), lambda i, j: (0, j)),
                  pl.BlockSpec((tf, d), lambda i, j: (j, 0)),
                  pl.BlockSpec((1, d), lambda i, j: (0, 0)),
                  pl.BlockSpec((1, d), lambda i, j: (0, 0))],
        out_specs=pl.BlockSpec(memory_space=pl.ANY),
        scratch_shapes=[pltpu.VMEM((tm, d), F32), pltpu.VMEM((tm, d), BF16),
                        pltpu.SemaphoreType.DMA((pl.cdiv(tm, min(tm, NORM_ROWS)),))],
        compiler_params=_params(("arbitrary", "arbitrary")),
        name="swiglu_ffn",
    )(h, wg, wu, wd, g_in.reshape(1, d), g_out.reshape(1, d))


def kernel(x, g_mix, w_in, w_pool_group, pool_scale, lb_param, hgrn_norm, w_up_pool, w_up_hgrn, w_out,
           g_ffn, w_ffn_gate, w_ffn_up, w_ffn_down, g_final):
    bsz, t, d = x.shape
    depth = g_mix.shape[0]
    assert d % (2 * HEAD_DIM) == 0 and t % CHUNK == 0
    outs = []
    for bi in range(bsz):
        h = x[bi]
        for l in range(depth):
            act = _proj(h, g_mix[l], w_in[l], lb_param, l)
            y_pool = _pool(act, w_pool_group[l], pool_scale[l])
            y_hgrn = _hgrn(act, hgrn_norm[l], d)
            merged = _merge(y_pool, y_hgrn, w_up_pool[l], w_up_hgrn[l], act, d)
            h1 = _outproj(merged, w_out[l], h)
            h = _ffn(h1, w_ffn_gate[l], w_ffn_up[l], w_ffn_down[l], g_ffn[l], g_final, l == depth - 1)
        outs.append(h)
    return jnp.stack(outs, axis=0)
```

```python
import functools

import numpy as np
import jax
import jax.numpy as jnp
from jax import lax
from jax.experimental import pallas as pl
from jax.experimental.pallas import tpu as pltpu

EPS = 1e-6
POOL_WINDOWS = (2, 4, 8, 16)
POOL_HALO = 16
HEAD_DIM = 128
CHUNK = 64
NORM_ROWS = 256
PROJ_SLAB_ROWS = 512
OUTPROJ_SLAB_ROWS = 128
HGRN_HEADS_PER_STEP = 16
HGRN_CHUNKS_PER_ITER = 2
LEVELS = (32, 16, 8, 4, 2)

LOG2_E = 1.4426950408889634

F32 = jnp.float32
BF16 = jnp.bfloat16

VMEM_LIMIT_BYTES = 60 * 1024 * 1024


def _tile(dim, pref, align=128):
    if dim <= pref:
        return dim
    t = (pref // align) * align
    while t >= align:
        if dim % t == 0:
            return t
        t -= align
    return dim


def _params(sem):
    return pltpu.CompilerParams(dimension_semantics=sem, vmem_limit_bytes=VMEM_LIMIT_BYTES)


def _slabs(rows, slab):
    n = min(rows, slab)
    assert rows % n == 0
    return [slice(r, r + n) for r in range(0, rows, n)]


def _sigmoid(x):
    return 1.0 / (1.0 + jnp.exp(-x))


def _rms_scale(h):
    return lax.rsqrt(jnp.mean(h * h, axis=-1, keepdims=True) + EPS)


def _rms_scale_ref(ref, r, n):
    half = ref.shape[1] // 2
    lo = ref[r:r + n, :half]
    hi = ref[r:r + n, half:]
    ss = jnp.sum(lo * lo, axis=-1, keepdims=True) + jnp.sum(hi * hi, axis=-1, keepdims=True)
    return lax.rsqrt(ss / ref.shape[1] + EPS)


def _proj_kernel(x_hbm, g_ref, w_ref, lbp_ref, o_ref, u_ref, xbuf, sem, *, layer, tn, half):
    i = pl.program_id(0)
    tm = u_ref.shape[0]
    rc = xbuf.shape[1]

    def xcopy(c):
        src = x_hbm.at[pl.ds(pl.multiple_of(i * tm, tm) + c * rc, rc), :]
        return pltpu.make_async_copy(src, xbuf.at[c % 2], sem.at[c % 2])

    @pl.when(pl.program_id(1) == 0)
    def _():
        nchunks = tm // rc
        for c in range(min(2, nchunks)):
            xcopy(c).start()
        for c in range(nchunks):
            xcopy(c).wait()
            slot = xbuf.at[c % 2]
            scale = _rms_scale_ref(slot, 0, rc)
            u_ref[c * rc:(c + 1) * rc, :] = (slot[...] * scale * g_ref[...]).astype(BF16)
            if c + 2 < nchunks:
                xcopy(c + 2).start()

    seg = (pl.program_id(1) * tn) // half
    p = lbp_ref[...]
    e = jnp.exp(p - jnp.max(p, axis=0, keepdims=True))
    lb = jnp.sum(e[: layer + 1], axis=0, keepdims=True) / jnp.sum(e, axis=0, keepdims=True)
    is_f = seg == 2
    is_silu = jnp.logical_or(seg == 1, seg == 4)
    c0 = jnp.where(is_f, lb, 0.0)
    c1 = jnp.where(jnp.logical_or(seg == 0, seg == 3), 1.0, 0.0)
    c2 = jnp.where(is_f, 1.0 - lb, jnp.where(seg >= 5, 1.0, 0.0))
    c3 = jnp.where(is_silu, 1.0, 0.0)
    w = w_ref[...].astype(BF16)
    for rows in _slabs(tm, PROJ_SLAB_ROWS):
        acc = jnp.dot(u_ref[rows, :], w, preferred_element_type=F32)
        o_ref[rows, :] = c0 + c1 * acc + _sigmoid(acc) * (c2 + c3 * acc)


def _proj(x, g, w, lb_param, layer):
    t, d = x.shape
    n_in = w.shape[1]
    half = d // 2
    tm = _tile(t, 2048, 8)
    tn = _tile(half, 512)
    rc = _tile(tm, NORM_ROWS, 8)
    nb = half // tn
    return pl.pallas_call(
        functools.partial(_proj_kernel, layer=layer, tn=tn, half=half),
        out_shape=jax.ShapeDtypeStruct((t, n_in), F32),
        grid=(t // tm, n_in // tn),
        in_specs=[pl.BlockSpec(memory_space=pl.ANY),
                  pl.BlockSpec((1, d), lambda i, j: (0, 0)),
                  pl.BlockSpec((d, tn), lambda i, j: (0, j)),
                  pl.BlockSpec((lb_param.shape[0], tn), lambda i, j: (0, jnp.clip(j - 2 * nb, 0, nb - 1)))],
        out_specs=pl.BlockSpec((tm, tn), lambda i, j: (i, j)),
        scratch_shapes=[pltpu.VMEM((tm, d), BF16), pltpu.VMEM((2, rc, d), F32), pltpu.SemaphoreType.DMA((2,))],
        compiler_params=_params(("arbitrary", "arbitrary")),
        name="in_proj",
    )(x, g.reshape(1, d), w, lb_param)


def _pool_kernel(z_ref, halo_ref, w_ref, s_ref, o_ref, *, tp):
    i = pl.program_id(0)
    g = pl.program_id(1)
    z = z_ref[...]
    halo = jnp.where(i > 0, halo_ref[...], 0.0)
    full = jnp.concatenate([halo, z], axis=0)
    t = i * tp + lax.broadcasted_iota(jnp.int32, (tp, 1), 0)

    for gi, w in enumerate(POOL_WINDOWS):
        @pl.when(g == gi)
        def _(w=w):
            s = full
            k = 1
            while k < w:
                s = s + pltpu.roll(s, k, 0)
                k *= 2
            count = jnp.minimum(t + 1, w).astype(F32)
            pooled = s[POOL_HALO:] / count - z
            y = jnp.dot(pooled.astype(BF16), w_ref[0].astype(BF16), preferred_element_type=F32)
            o_ref[...] = (y * s_ref[0]).astype(o_ref.dtype)


def _pool(act, w_group, scale):
    t = act.shape[0]
    ng, gsz, _ = w_group.shape
    assert ng == len(POOL_WINDOWS)
    tp = _tile(t, 2048, POOL_HALO)
    hb = tp // POOL_HALO
    return pl.pallas_call(
        functools.partial(_pool_kernel, tp=tp),
        out_shape=jax.ShapeDtypeStruct((t, ng * gsz), BF16),
        grid=(t // tp, ng),
        in_specs=[pl.BlockSpec((tp, gsz), lambda i, g: (i, g)),
                  pl.BlockSpec((POOL_HALO, gsz), lambda i, g: (jnp.maximum(i * hb - 1, 0), g)),
                  pl.BlockSpec((1, gsz, gsz), lambda i, g: (g, 0, 0)),
                  pl.BlockSpec((1, 1, gsz), lambda i, g: (g, 0, 0))],
        out_specs=pl.BlockSpec((tp, gsz), lambda i, g: (i, g)),
        compiler_params=_params(("parallel", "arbitrary")),
        name="pool_mixer",
    )(act, act, w_group, scale.reshape(ng, 1, gsz))


def _level_ids():
    t = np.arange(CHUNK)[:, None]
    s = np.arange(CHUNK)[None, :]
    ids = -np.ones((CHUNK, CHUNK), np.int32)
    ids[t == s] = 0
    for lid, n in enumerate((1,) + tuple(reversed(LEVELS)), start=1):
        m = (t // (2 * n) == s // (2 * n)) & (t % (2 * n) >= n) & (s % (2 * n) < n)
        ids[m] = lid
    return ids


def _nt_dot(a, b):
    return lax.dot_general(a, b, (((1,), (1,)), ((), ())), preferred_element_type=F32)


def _tn_dot(a, b):
    return lax.dot_general(a, b, (((0,), (0,)), ((), ())), preferred_element_type=F32)


def _hgrn_kernel(q_ref, f_ref, v_ref, og_ref, gn_ref, ids_ref, o_ref, st_ref, b_ref, *, nh, nchunk):
    @pl.when(pl.program_id(1) == 0)
    def _():
        st_ref[...] = jnp.zeros_like(st_ref)

    ids = ids_ref[...]
    sub = lax.broadcasted_iota(jnp.int32, (8, HEAD_DIM), 0)
    row = lax.broadcasted_iota(jnp.int32, (CHUNK, HEAD_DIM), 0)
    tril = (lax.broadcasted_iota(jnp.int32, (CHUNK, CHUNK), 0)
            >= lax.broadcasted_iota(jnp.int32, (CHUNK, CHUNK), 1)).astype(BF16)

    def bcast_row(h, r, n):
        return jnp.broadcast_to(b_ref[h, r:r + 1, :], (n, HEAD_DIM))

    def neg_abs(x):
        return pltpu.bitcast(pltpu.bitcast(x, jnp.uint32) | jnp.uint32(0x80000000), F32)

    def chunk_body(c, carry):
        heads = range(nh * HGRN_CHUNKS_PER_ITER)
        starts = [pl.multiple_of((c * HGRN_CHUNKS_PER_ITER + h // nh) * CHUNK, CHUNK) for h in heads]
        rows = [pl.ds(starts[h], CHUNK) for h in heads]
        lanes = [slice((h % nh) * HEAD_DIM, (h % nh + 1) * HEAD_DIM) for h in heads]
        def qv(h, a=0, n=CHUNK):
            return q_ref[pl.ds(starts[h] + a, n), lanes[h]]

        def fv(h, a=0, n=CHUNK):
            return f_ref[pl.ds(starts[h] + a, n), lanes[h]]

        def kv(h, a=0, n=CHUNK):
            return 1.0 - fv(h, a, n)

        def bv(h, a=0, n=CHUNK):
            return b_ref[h, a:a + n, :]

        scores = {}
        out = {}

        def stage_decay(h):
            g = jnp.log(fv(h)) * LOG2_E
            g_hi = g.astype(BF16)
            g_r = g - g_hi.astype(F32)
            g_mid = g_r.astype(BF16)
            g_lo = (g_r - g_mid.astype(F32)).astype(BF16)
            b_ref[h] = (jnp.dot(tril, g_hi, preferred_element_type=F32)
                        + jnp.dot(tril, g_mid, preferred_element_type=F32)
                        + jnp.dot(tril, g_lo, preferred_element_type=F32))

        def stage_diagonal(h):
            k_bf = kv(h).astype(BF16)
            s = jnp.where(ids == 0, _nt_dot(qv(h).astype(BF16), k_bf), 0.0)
            scores[h] = jnp.where(ids == 1, _nt_dot((qv(h) * fv(h)).astype(BF16), k_bf), s)

        def stage_level(h, lid, n):
            if n >= 8:
                d_parts, x_parts = [], []
                for base in range(0, CHUNK, 2 * n):
                    mid = bcast_row(h, base + n - 1, n)
                    d_parts += [mid - bv(h, base, n), bv(h, base + n, n) - mid]
                    x_parts += [kv(h, base, n), qv(h, base + n, n)]
                d = jnp.concatenate(d_parts, axis=0)
                xs = jnp.concatenate(x_parts, axis=0)
            else:
                if n == 4:
                    parts = [bcast_row(h, 8 * i + 3, 8) for i in range(CHUNK // 8)]
                else:
                    parts = [jnp.where(sub < 4, bcast_row(h, 8 * i + 1, 8), bcast_row(h, 8 * i + 5, 8))
                             for i in range(CHUNK // 8)]
                d = neg_abs(bv(h) - jnp.concatenate(parts, axis=0))
                xs = jnp.where((row & n) != 0, qv(h), kv(h))
            x = (xs * jnp.exp2(d)).astype(BF16)
            scores[h] = jnp.where(ids == lid, _nt_dot(x, x), scores[h])

        def stage_state(h):
            b_end = bcast_row(h, CHUNK - 1, CHUNK)
            st = st_ref[h % nh]
            vh = v_ref[rows[h], lanes[h]].astype(BF16)
            oh = jnp.dot(scores.pop(h).astype(BF16), vh, preferred_element_type=F32)
            out[h] = oh + _nt_dot((qv(h) * jnp.exp2(bv(h))).astype(BF16), st.astype(BF16))
            k_dec = (kv(h) * jnp.exp2(b_end - bv(h))).astype(BF16)
            st_ref[h % nh] = st * jnp.exp2(b_end[:1]) + _tn_dot(vh, k_dec)

        def stage_output(h):
            o = out.pop(h)
            y = o * _rms_scale(o) * gn_ref[:, lanes[h]] * og_ref[rows[h], lanes[h]]
            o_ref[rows[h], lanes[h]] = y.astype(o_ref.dtype)

        stages = [stage_decay, stage_diagonal]
        stages += [functools.partial(stage_level, lid=lid, n=n) for lid, n in enumerate(reversed(LEVELS), start=2)]
        stages += [stage_state, stage_output]
        for stage in stages:
            for h in heads:
                stage(h)
        return carry

    lax.fori_loop(0, nchunk // HGRN_CHUNKS_PER_ITER, chunk_body, 0)


def _hgrn(act, gnorm, d):
    t = act.shape[0]
    width = d // 2
    nh = min(HGRN_HEADS_PER_STEP, width // HEAD_DIM)
    bw = nh * HEAD_DIM
    ngroups = width // bw
    tr = _tile(t, 512, CHUNK * HGRN_CHUNKS_PER_ITER)
    assert tr % (CHUNK * HGRN_CHUNKS_PER_ITER) == 0

    def seg_spec(seg):
        return pl.BlockSpec((tr, bw), lambda g, r, seg=seg: (r, seg * ngroups + g))

    return pl.pallas_call(
        functools.partial(_hgrn_kernel, nh=nh, nchunk=tr // CHUNK),
        out_shape=jax.ShapeDtypeStruct((t, width), BF16),
        grid=(ngroups, t // tr),
        in_specs=[seg_spec(1), seg_spec(2), seg_spec(3), seg_spec(4),
                  pl.BlockSpec((1, bw), lambda g, r: (0, g)),
                  pl.BlockSpec((CHUNK, CHUNK), lambda g, r: (0, 0))],
        out_specs=pl.BlockSpec((tr, bw), lambda g, r: (r, g)),
        scratch_shapes=[pltpu.VMEM((nh, HEAD_DIM, HEAD_DIM), F32),
                        pltpu.VMEM((nh * HGRN_CHUNKS_PER_ITER, CHUNK, HEAD_DIM), F32)],
        compiler_params=_params(("parallel", "arbitrary")),
        name="hgrn2",
    )(act, act, act, act, gnorm.reshape(1, width), jnp.asarray(_level_ids()))


def _merge_kernel(yp_ref, yh_ref, wp_ref, wh_ref, ga_ref, gb_ref, o_ref):
    a = jnp.dot(yp_ref[...], wp_ref[...].astype(BF16), preferred_element_type=F32)
    b = jnp.dot(yh_ref[...], wh_ref[...].astype(BF16), preferred_element_type=F32)
    o_ref[...] = (ga_ref[...] * a + gb_ref[...] * b).astype(o_ref.dtype)


def _merge(y_pool, y_hgrn, w_up_pool, w_up_hgrn, act, d):
    t, kp = y_pool.shape
    kh = y_hgrn.shape[1]
    tm = _tile(t, 2048, 8)
    tn = _tile(d // 2, 256)
    nb = d // tn
    off_a = (act.shape[1] - 2 * d) // tn
    return pl.pallas_call(
        _merge_kernel,
        out_shape=jax.ShapeDtypeStruct((t, d), BF16),
        grid=(t // tm, nb),
        in_specs=[pl.BlockSpec((tm, kp), lambda i, j: (i, 0)),
                  pl.BlockSpec((tm, kh), lambda i, j: (i, 0)),
                  pl.BlockSpec((kp, tn), lambda i, j: (0, j)),
                  pl.BlockSpec((kh, tn), lambda i, j: (0, j)),
                  pl.BlockSpec((tm, tn), lambda i, j: (i, off_a + j)),
                  pl.BlockSpec((tm, tn), lambda i, j: (i, off_a + nb + j))],
        out_specs=pl.BlockSpec((tm, tn), lambda i, j: (i, j)),
        compiler_params=_params(("parallel", "arbitrary")),
        name="gated_merge",
    )(y_pool, y_hgrn, w_up_pool, w_up_hgrn, act, act)


def _outproj_kernel(m_ref, w_ref, x_ref, h_ref):
    w = w_ref[...].astype(BF16)
    for rows in _slabs(h_ref.shape[0], OUTPROJ_SLAB_ROWS):
        h_ref[rows, :] = x_ref[rows, :] + jnp.dot(m_ref[rows, :], w, preferred_element_type=F32)


def _outproj(m, w, x):
    t, d = x.shape
    tm = _tile(t, 2048, 8)
    tn = _tile(d, 256)
    return pl.pallas_call(
        _outproj_kernel,
        out_shape=jax.ShapeDtypeStruct((t, d), F32),
        grid=(t // tm, d // tn),
        in_specs=[pl.BlockSpec((tm, d), lambda i, j: (i, 0)),
                  pl.BlockSpec((d, tn), lambda i, j: (0, j)),
                  pl.BlockSpec((tm, tn), lambda i, j: (i, j))],
        out_specs=pl.BlockSpec((tm, tn), lambda i, j: (i, j)),
        compiler_params=_params(("parallel", "arbitrary")),
        name="out_proj",
    )(m, w, x)


def _ffn_kernel(h_hbm, wg_ref, wu_ref, wd_ref, gin_ref, gout_ref, o_hbm, acc_ref, v_ref, sem,
                *, nj, tm, final_norm):
    i = pl.program_id(0)
    j = pl.program_id(1)
    chunk = min(tm, NORM_ROWS)
    starts = range(0, tm, chunk)

    def copy(c, r, to_vmem):
        hbm = (h_hbm if to_vmem else o_hbm).at[pl.ds(pl.multiple_of(i * tm, tm) + r, chunk), :]
        vmem = acc_ref.at[r:r + chunk, :]
        return pltpu.make_async_copy(hbm, vmem, sem.at[c]) if to_vmem else \
            pltpu.make_async_copy(vmem, hbm, sem.at[c])

    @pl.when(j == 0)
    def _():
        for c, r in enumerate(starts):
            copy(c, r, True).start()
        for c, r in enumerate(starts):
            copy(c, r, True).wait()
            scale = _rms_scale_ref(acc_ref, r, chunk)
            v_ref[r:r + chunk, :] = (acc_ref[r:r + chunk, :] * scale * gin_ref[...]).astype(BF16)

    v = v_ref[...]
    gate = jnp.dot(v, wg_ref[...].astype(BF16), preferred_element_type=F32)
    up = jnp.dot(v, wu_ref[...].astype(BF16), preferred_element_type=F32)
    a = (gate * _sigmoid(gate) * up).astype(BF16)
    acc_ref[...] += jnp.dot(a, wd_ref[...].astype(BF16), preferred_element_type=F32)

    @pl.when(j == nj - 1)
    def _():
        for c, r in enumerate(starts):
            if final_norm:
                scale = _rms_scale_ref(acc_ref, r, chunk)
                acc_ref[r:r + chunk, :] = acc_ref[r:r + chunk, :] * scale * gout_ref[...]
            copy(c, r, False).start()
        for c, r in enumerate(starts):
            copy(c, r, False).wait()


def _ffn(h, wg, wu, wd, g_in, g_out, final_norm):
    t, d = h.shape
    ff = wd.shape[0]
    tm = _tile(t, 1024, 8)
    tf = _tile(ff, 256)
    nj = ff // tf
    return pl.pallas_call(
        functools.partial(_ffn_kernel, nj=nj, tm=tm, final_norm=final_norm),
        out_shape=jax.ShapeDtypeStruct((t, d), F32),
        grid=(t // tm, nj),
        in_specs=[pl.BlockSpec(memory_space=pl.ANY),
                  pl.BlockSpec((d, tf), lambda i, j: (0, j)),
                  pl.BlockSpec((d, tf), lambda i, j: (0, j)),
                  pl.BlockSpec((tf, d), lambda i, j: (j, 0)),
                  pl.BlockSpec((1, d), lambda i, j: (0, 0)),
                  pl.BlockSpec((1, d), lambda i, j: (0, 0))],
        out_specs=pl.BlockSpec(memory_space=pl.ANY),
        scratch_shapes=[pltpu.VMEM((tm, d), F32), pltpu.VMEM((tm, d), BF16),
                        pltpu.SemaphoreType.DMA((pl.cdiv(tm, min(tm, NORM_ROWS)),))],
        compiler_params=_params(("arbitrary", "arbitrary")),
        name="swiglu_ffn",
    )(h, wg, wu, wd, g_in.reshape(1, d), g_out.reshape(1, d))


def kernel(x, g_mix, w_in, w_pool_group, pool_scale, lb_param, hgrn_norm, w_up_pool, w_up_hgrn, w_out,
           g_ffn, w_ffn_gate, w_ffn_up, w_ffn_down, g_final):
    bsz, t, d = x.shape
    depth = g_mix.shape[0]
    assert d % (2 * HEAD_DIM) == 0 and t % CHUNK == 0
    outs = []
    for bi in range(bsz):
        h = x[bi]
        for l in range(depth):
            act = _proj(h, g_mix[l], w_in[l], lb_param, l)
            y_pool = _pool(act, w_pool_group[l], pool_scale[l])
            y_hgrn = _hgrn(act, hgrn_norm[l], d)
            merged = _merge(y_pool, y_hgrn, w_up_pool[l], w_up_hgrn[l], act, d)
            h1 = _outproj(merged, w_out[l], h)
            h = _ffn(h1, w_ffn_gate[l], w_ffn_up[l], w_ffn_down[l], g_ffn[l], g_final, l == depth - 1)
        outs.append(h)
    return jnp.stack(outs, axis=0)
```

```python
import functools

import numpy as np
import jax
import jax.numpy as jnp
from jax import lax
from jax.experimental import pallas as pl
from jax.experimental.pallas import tpu as pltpu

EPS = 1e-6
POOL_WINDOWS = (2, 4, 8, 16)
POOL_HALO = 16
HEAD_DIM = 128
CHUNK = 64
NORM_ROWS = 256
PROJ_SLAB_ROWS = 512
OUTPROJ_SLAB_ROWS = 128
HGRN_HEADS_PER_STEP = 16
HGRN_CHUNKS_PER_ITER = 2
LEVELS = (32, 16, 8, 4, 2)

LOG2_E = 1.4426950408889634

F32 = jnp.float32
BF16 = jnp.bfloat16

VMEM_LIMIT_BYTES = 60 * 1024 * 1024


def _tile(dim, pref, align=128):
    if dim <= pref:
        return dim
    t = (pref // align) * align
    while t >= align:
        if dim % t == 0:
            return t
        t -= align
    return dim


def _params(sem):
    return pltpu.CompilerParams(dimension_semantics=sem, vmem_limit_bytes=VMEM_LIMIT_BYTES)


def _slabs(rows, slab):
    n = min(rows, slab)
    assert rows % n == 0
    return [slice(r, r + n) for r in range(0, rows, n)]


def _sigmoid(x):
    return 1.0 / (1.0 + jnp.exp(-x))


def _rms_scale(h):
    return lax.rsqrt(jnp.mean(h * h, axis=-1, keepdims=True) + EPS)


def _rms_scale_ref(ref, r, n):
    half = ref.shape[1] // 2
    lo = ref[r:r + n, :half]
    hi = ref[r:r + n, half:]
    ss = jnp.sum(lo * lo, axis=-1, keepdims=True) + jnp.sum(hi * hi, axis=-1, keepdims=True)
    return lax.rsqrt(ss / ref.shape[1] + EPS)


def _proj_kernel(x_hbm, g_ref, w_ref, lbp_ref, o_ref, u_ref, xbuf, sem, *, layer, tn, half):
    i = pl.program_id(0)
    tm = u_ref.shape[0]
    rc = xbuf.shape[1]

    def xcopy(c):
        src = x_hbm.at[pl.ds(pl.multiple_of(i * tm, tm) + c * rc, rc), :]
        return pltpu.make_async_copy(src, xbuf.at[c % 2], sem.at[c % 2])

    @pl.when(pl.program_id(1) == 0)
    def _():
        nchunks = tm // rc
        for c in range(min(2, nchunks)):
            xcopy(c).start()
        for c in range(nchunks):
            xcopy(c).wait()
            slot = xbuf.at[c % 2]
            scale = _rms_scale_ref(slot, 0, rc)
            u_ref[c * rc:(c + 1) * rc, :] = (slot[...] * scale * g_ref[...]).astype(BF16)
            if c + 2 < nchunks:
                xcopy(c + 2).start()

    seg = (pl.program_id(1) * tn) // half
    p = lbp_ref[...]
    e = jnp.exp(p - jnp.max(p, axis=0, keepdims=True))
    lb = jnp.sum(e[: layer + 1], axis=0, keepdims=True) / jnp.sum(e, axis=0, keepdims=True)
    is_f = seg == 2
    is_silu = jnp.logical_or(seg == 1, seg == 4)
    c0 = jnp.where(is_f, lb, 0.0)
    c1 = jnp.where(jnp.logical_or(seg == 0, seg == 3), 1.0, 0.0)
    c2 = jnp.where(is_f, 1.0 - lb, jnp.where(seg >= 5, 1.0, 0.0))
    c3 = jnp.where(is_silu, 1.0, 0.0)
    w = w_ref[...].astype(BF16)
    for rows in _slabs(tm, PROJ_SLAB_ROWS):
        acc = jnp.dot(u_ref[rows, :], w, preferred_element_type=F32)
        o_ref[rows, :] = c0 + c1 * acc + _sigmoid(acc) * (c2 + c3 * acc)


def _proj(x, g, w, lb_param, layer):
    t, d = x.shape
    n_in = w.shape[1]
    half = d // 2
    tm = _tile(t, 2048, 8)
    tn = _tile(half, 512)
    rc = _tile(tm, NORM_ROWS, 8)
    nb = half // tn
    return pl.pallas_call(
        functools.partial(_proj_kernel, layer=layer, tn=tn, half=half),
        out_shape=jax.ShapeDtypeStruct((t, n_in), F32),
        grid=(t // tm, n_in // tn),
        in_specs=[pl.BlockSpec(memory_space=pl.ANY),
                  pl.BlockSpec((1, d), lambda i, j: (0, 0)),
                  pl.BlockSpec((d, tn), lambda i, j: (0, j)),
                  pl.BlockSpec((lb_param.shape[0], tn), lambda i, j: (0, jnp.clip(j - 2 * nb, 0, nb - 1)))],
        out_specs=pl.BlockSpec((tm, tn), lambda i, j: (i, j)),
        scratch_shapes=[pltpu.VMEM((tm, d), BF16), pltpu.VMEM((2, rc, d), F32), pltpu.SemaphoreType.DMA((2,))],
        compiler_params=_params(("arbitrary", "arbitrary")),
        name="in_proj",
    )(x, g.reshape(1, d), w, lb_param)


def _pool_kernel(z_ref, halo_ref, w_ref, s_ref, o_ref, *, tp):
    i = pl.program_id(0)
    g = pl.program_id(1)
    z = z_ref[...]
    halo = jnp.where(i > 0, halo_ref[...], 0.0)
    full = jnp.concatenate([halo, z], axis=0)
    t = i * tp + lax.broadcasted_iota(jnp.int32, (tp, 1), 0)

    for gi, w in enumerate(POOL_WINDOWS):
        @pl.when(g == gi)
        def _(w=w):
            s = full
            k = 1
            while k < w:
                s = s + pltpu.roll(s, k, 0)
                k *= 2
            count = jnp.minimum(t + 1, w).astype(F32)
            pooled = s[POOL_HALO:] / count - z
            y = jnp.dot(pooled.astype(BF16), w_ref[0].astype(BF16), preferred_element_type=F32)
            o_ref[...] = (y * s_ref[0]).astype(o_ref.dtype)


def _pool(act, w_group, scale):
    t = act.shape[0]
    ng, gsz, _ = w_group.shape
    assert ng == len(POOL_WINDOWS)
    tp = _tile(t, 2048, POOL_HALO)
    hb = tp // POOL_HALO
    return pl.pallas_call(
        functools.partial(_pool_kernel, tp=tp),
        out_shape=jax.ShapeDtypeStruct((t, ng * gsz), BF16),
        grid=(t // tp, ng),
        in_specs=[pl.BlockSpec((tp, gsz), lambda i, g: (i, g)),
                  pl.BlockSpec((POOL_HALO, gsz), lambda i, g: (jnp.maximum(i * hb - 1, 0), g)),
                  pl.BlockSpec((1, gsz, gsz), lambda i, g: (g, 0, 0)),
                  pl.BlockSpec((1, 1, gsz), lambda i, g: (g, 0, 0))],
        out_specs=pl.BlockSpec((tp, gsz), lambda i, g: (i, g)),
        compiler_params=_params(("parallel", "arbitrary")),
        name="pool_mixer",
    )(act, act, w_group, scale.reshape(ng, 1, gsz))


def _level_ids():
    t = np.arange(CHUNK)[:, None]
    s = np.arange(CHUNK)[None, :]
    ids = -np.ones((CHUNK, CHUNK), np.int32)
    ids[t == s] = 0
    for lid, n in enumerate((1,) + tuple(reversed(LEVELS)), start=1):
        m = (t // (2 * n) == s // (2 * n)) & (t % (2 * n) >= n) & (s % (2 * n) < n)
        ids[m] = lid
    return ids


def _nt_dot(a, b):
    return lax.dot_general(a, b, (((1,), (1,)), ((), ())), preferred_element_type=F32)


def _tn_dot(a, b):
    return lax.dot_general(a, b, (((0,), (0,)), ((), ())), preferred_element_type=F32)


def _hgrn_kernel(q_ref, f_ref, v_ref, og_ref, gn_ref, ids_ref, o_ref, st_ref, b_ref, *, nh, nchunk):
    @pl.when(pl.program_id(1) == 0)
    def _():
        st_ref[...] = jnp.zeros_like(st_ref)

    ids = ids_ref[...]
    sub = lax.broadcasted_iota(jnp.int32, (8, HEAD_DIM), 0)
    row = lax.broadcasted_iota(jnp.int32, (CHUNK, HEAD_DIM), 0)
    tril = (lax.broadcasted_iota(jnp.int32, (CHUNK, CHUNK), 0)
            >= lax.broadcasted_iota(jnp.int32, (CHUNK, CHUNK), 1)).astype(BF16)

    def bcast_row(h, r, n):
        return jnp.broadcast_to(b_ref[h, r:r + 1, :], (n, HEAD_DIM))

    def neg_abs(x):
        return pltpu.bitcast(pltpu.bitcast(x, jnp.uint32) | jnp.uint32(0x80000000), F32)

    def chunk_body(c, carry):
        heads = range(nh * HGRN_CHUNKS_PER_ITER)
        starts = [pl.multiple_of((c * HGRN_CHUNKS_PER_ITER + h // nh) * CHUNK, CHUNK) for h in heads]
        rows = [pl.ds(starts[h], CHUNK) for h in heads]
        lanes = [slice((h % nh) * HEAD_DIM, (h % nh + 1) * HEAD_DIM) for h in heads]
        def qv(h, a=0, n=CHUNK):
            return q_ref[pl.ds(starts[h] + a, n), lanes[h]]

        def fv(h, a=0, n=CHUNK):
            return f_ref[pl.ds(starts[h] + a, n), lanes[h]]

        def kv(h, a=0, n=CHUNK):
            return 1.0 - fv(h, a, n)

        def bv(h, a=0, n=CHUNK):
            return b_ref[h, a:a + n, :]

        scores = {}
        out = {}

        def stage_decay(h):
            g = jnp.log(fv(h)) * LOG2_E
            g_hi = g.astype(BF16)
            g_r = g - g_hi.astype(F32)
            g_mid = g_r.astype(BF16)
            g_lo = (g_r - g_mid.astype(F32)).astype(BF16)
            b_ref[h] = (jnp.dot(tril, g_hi, preferred_element_type=F32)
                        + jnp.dot(tril, g_mid, preferred_element_type=F32)
                        + jnp.dot(tril, g_lo, preferred_element_type=F32))

        def stage_diagonal(h):
            k_bf = kv(h).astype(BF16)
            s = jnp.where(ids == 0, _nt_dot(qv(h).astype(BF16), k_bf), 0.0)
            scores[h] = jnp.where(ids == 1, _nt_dot((qv(h) * fv(h)).astype(BF16), k_bf), s)

        def stage_level(h, lid, n):
            if n >= 8:
                d_parts, x_parts = [], []
                for base in range(0, CHUNK, 2 * n):
                    mid = bcast_row(h, base + n - 1, n)
                    d_parts += [mid - bv(h, base, n), bv(h, base + n, n) - mid]
                    x_parts += [kv(h, base, n), qv(h, base + n, n)]
                d = jnp.concatenate(d_parts, axis=0)
                xs = jnp.concatenate(x_parts, axis=0)
            else:
                if n == 4:
                    parts = [bcast_row(h, 8 * i + 3, 8) for i in range(CHUNK // 8)]
                else:
                    parts = [jnp.where(sub < 4, bcast_row(h, 8 * i + 1, 8), bcast_row(h, 8 * i + 5, 8))
                             for i in range(CHUNK // 8)]
                d = neg_abs(bv(h) - jnp.concatenate(parts, axis=0))
                xs = jnp.where((row & n) != 0, qv(h), kv(h))
            x = (xs * jnp.exp2(d)).astype(BF16)
            scores[h] = jnp.where(ids == lid, _nt_dot(x, x), scores[h])

        def stage_state(h):
            b_end = bcast_row(h, CHUNK - 1, CHUNK)
            st = st_ref[h % nh]
            vh = v_ref[rows[h], lanes[h]].astype(BF16)
            oh = jnp.dot(scores.pop(h).astype(BF16), vh, preferred_element_type=F32)
            out[h] = oh + _nt_dot((qv(h) * jnp.exp2(bv(h))).astype(BF16), st.astype(BF16))
            k_dec = (kv(h) * jnp.exp2(b_end - bv(h))).astype(BF16)
            st_ref[h % nh] = st * jnp.exp2(b_end[:1]) + _tn_dot(vh, k_dec)

        def stage_output(h):
            o = out.pop(h)
            y = o * _rms_scale(o) * gn_ref[:, lanes[h]] * og_ref[rows[h], lanes[h]]
            o_ref[rows[h], lanes[h]] = y.astype(o_ref.dtype)

        stages = [stage_decay, stage_diagonal]
        stages += [functools.partial(stage_level, lid=lid, n=n) for lid, n in enumerate(reversed(LEVELS), start=2)]
        stages += [stage_state, stage_output]
        for stage in stages:
            for h in heads:
                stage(h)
        return carry

    lax.fori_loop(0, nchunk // HGRN_CHUNKS_PER_ITER, chunk_body, 0)


def _hgrn(act, gnorm, d):
    t = act.shape[0]
    width = d // 2
    nh = min(HGRN_HEADS_PER_STEP, width // HEAD_DIM)
    bw = nh * HEAD_DIM
    ngroups = width // bw
    tr = _tile(t, 512, CHUNK * HGRN_CHUNKS_PER_ITER)
    assert tr % (CHUNK * HGRN_CHUNKS_PER_ITER) == 0

    def seg_spec(seg):
        return pl.BlockSpec((tr, bw), lambda g, r, seg=seg: (r, seg * ngroups + g))

    return pl.pallas_call(
        functools.partial(_hgrn_kernel, nh=nh, nchunk=tr // CHUNK),
        out_shape=jax.ShapeDtypeStruct((t, width), BF16),
        grid=(ngroups, t // tr),
        in_specs=[seg_spec(1), seg_spec(2), seg_spec(3), seg_spec(4),
                  pl.BlockSpec((1, bw), lambda g, r: (0, g)),
                  pl.BlockSpec((CHUNK, CHUNK), lambda g, r: (0, 0))],
        out_specs=pl.BlockSpec((tr, bw), lambda g, r: (r, g)),
        scratch_shapes=[pltpu.VMEM((nh, HEAD_DIM, HEAD_DIM), F32),
                        pltpu.VMEM((nh * HGRN_CHUNKS_PER_ITER, CHUNK, HEAD_DIM), F32)],
        compiler_params=_params(("parallel", "arbitrary")),
        name="hgrn2",
    )(act, act, act, act, gnorm.reshape(1, width), jnp.asarray(_level_ids()))


def _merge_out_kernel(yp_ref, yh_ref, wp_ref, wh_ref, ga_ref, gb_ref, wo_ref, x_ref, h_ref, m_ref, *, nb):
    j = pl.program_id(1)

    @pl.when(j < nb)
    def _():
        a = jnp.dot(yp_ref[...], wp_ref[...].astype(BF16), preferred_element_type=F32)
        b = jnp.dot(yh_ref[...], wh_ref[...].astype(BF16), preferred_element_type=F32)
        m_ref[j] = (ga_ref[...] * a + gb_ref[...] * b).astype(BF16)

    @pl.when(j >= nb)
    def _():
        w = wo_ref[...].astype(BF16)
        for rows in _slabs(h_ref.shape[0], OUTPROJ_SLAB_ROWS):
            m = jnp.concatenate([m_ref[jj, rows, :] for jj in range(nb)], axis=1)
            h_ref[rows, :] = x_ref[rows, :] + jnp.dot(m, w, preferred_element_type=F32)


def _merge_out(y_pool, y_hgrn, w_up_pool, w_up_hgrn, act, w_out, x):
    t, d = x.shape
    kp, kh = y_pool.shape[1], y_hgrn.shape[1]
    tm = _tile(t, 1024, 8)
    tn = _tile(d // 2, 256)
    nb = d // tn
    off_a = (act.shape[1] - 2 * d) // tn

    def first(j):
        return jnp.minimum(j, nb - 1)

    def second(j):
        return jnp.maximum(j - nb, 0)

    return pl.pallas_call(
        functools.partial(_merge_out_kernel, nb=nb),
        out_shape=jax.ShapeDtypeStruct((t, d), F32),
        grid=(t // tm, 2 * nb),
        in_specs=[pl.BlockSpec((tm, kp), lambda i, j: (i, 0)),
                  pl.BlockSpec((tm, kh), lambda i, j: (i, 0)),
                  pl.BlockSpec((kp, tn), lambda i, j: (0, first(j))),
                  pl.BlockSpec((kh, tn), lambda i, j: (0, first(j))),
                  pl.BlockSpec((tm, tn), lambda i, j: (i, off_a + first(j))),
                  pl.BlockSpec((tm, tn), lambda i, j: (i, off_a + nb + first(j))),
                  pl.BlockSpec((d, tn), lambda i, j: (0, second(j))),
                  pl.BlockSpec((tm, tn), lambda i, j: (i, second(j)))],
        out_specs=pl.BlockSpec((tm, tn), lambda i, j: (i, second(j))),
        scratch_shapes=[pltpu.VMEM((nb, tm, tn), BF16)],
        compiler_params=_params(("arbitrary", "arbitrary")),
        name="merge_out_proj",
    )(y_pool, y_hgrn, w_up_pool, w_up_hgrn, act, act, w_out, x)


def _ffn_kernel(h_hbm, wg_ref, wu_ref, wd_ref, gin_ref, gout_ref, o_hbm, acc_ref, v_ref, sem,
                *, nj, tm, final_norm):
    i = pl.program_id(0)
    j = pl.program_id(1)
    chunk = min(tm, NORM_ROWS)
    starts = range(0, tm, chunk)

    def copy(c, r, to_vmem):
        hbm = (h_hbm if to_vmem else o_hbm).at[pl.ds(pl.multiple_of(i * tm, tm) + r, chunk), :]
        vmem = acc_ref.at[r:r + chunk, :]
        return pltpu.make_async_copy(hbm, vmem, sem.at[c]) if to_vmem else \
            pltpu.make_async_copy(vmem, hbm, sem.at[c])

    @pl.when(j == 0)
    def _():
        for c, r in enumerate(starts):
            copy(c, r, True).start()
        for c, r in enumerate(starts):
            copy(c, r, True).wait()
            scale = _rms_scale_ref(acc_ref, r, chunk)
            v_ref[r:r + chunk, :] = (acc_ref[r:r + chunk, :] * scale * gin_ref[...]).astype(BF16)

    v = v_ref[...]
    gate = jnp.dot(v, wg_ref[...].astype(BF16), preferred_element_type=F32)
    up = jnp.dot(v, wu_ref[...].astype(BF16), preferred_element_type=F32)
    a = (gate * _sigmoid(gate) * up).astype(BF16)
    acc_ref[...] += jnp.dot(a, wd_ref[...].astype(BF16), preferred_element_type=F32)

    @pl.when(j == nj - 1)
    def _():
        for c, r in enumerate(starts):
            if final_norm:
                scale = _rms_scale_ref(acc_ref, r, chunk)
                acc_ref[r:r + chunk, :] = acc_ref[r:r + chunk, :] * scale * gout_ref[...]
            copy(c, r, False).start()
        for c, r in enumerate(starts):
            copy(c, r, False).wait()


def _ffn(h, wg, wu, wd, g_in, g_out, final_norm):
    t, d = h.shape
    ff = wd.shape[0]
    tm = _tile(t, 1024, 8)
    tf = _tile(ff, 256)
    nj = ff // tf
    return pl.pallas_call(
        functools.partial(_ffn_kernel, nj=nj, tm=tm, final_norm=final_norm),
        out_shape=jax.ShapeDtypeStruct((t, d), F32),
        grid=(t // tm, nj),
        in_specs=[pl.BlockSpec(memory_space=pl.ANY),
                  pl.BlockSpec((d, tf), lambda i, j: (0, j)),
                  pl.BlockSpec((d, tf), lambda i, j: (0, j)),
                  pl.BlockSpec((tf, d), lambda i, j: (j, 0)),
                  pl.BlockSpec((1, d), lambda i, j: (0, 0)),
                  pl.BlockSpec((1, d), lambda i, j: (0, 0))],
        out_specs=pl.BlockSpec(memory_space=pl.ANY),
        scratch_shapes=[pltpu.VMEM((tm, d), F32), pltpu.VMEM((tm, d), BF16),
                        pltpu.SemaphoreType.DMA((pl.cdiv(tm, min(tm, NORM_ROWS)),))],
        compiler_params=_params(("arbitrary", "arbitrary")),
        name="swiglu_ffn",
    )(h, wg, wu, wd, g_in.reshape(1, d), g_out.reshape(1, d))


def kernel(x, g_mix, w_in, w_pool_group, pool_scale, lb_param, hgrn_norm, w_up_pool, w_up_hgrn, w_out,
           g_ffn, w_ffn_gate, w_ffn_up, w_ffn_down, g_final):
    bsz, t, d = x.shape
    depth = g_mix.shape[0]
    assert d % (2 * HEAD_DIM) == 0 and t % CHUNK == 0
    outs = []
    for bi in range(bsz):
        h = x[bi]
        for l in range(depth):
            act = _proj(h, g_mix[l], w_in[l], lb_param, l)
            y_pool = _pool(act, w_pool_group[l], pool_scale[l])
            y_hgrn = _hgrn(act, hgrn_norm[l], d)
            h1 = _merge_out(y_pool, y_hgrn, w_up_pool[l], w_up_hgrn[l], act, w_out[l], h)
            h = _ffn(h1, w_ffn_gate[l], w_ffn_up[l], w_ffn_down[l], g_ffn[l], g_final, l == depth - 1)
        outs.append(h)
    return jnp.stack(outs, axis=0)
```

```python
import functools

import numpy as np
import jax
import jax.numpy as jnp
from jax import lax
from jax.experimental import pallas as pl
from jax.experimental.pallas import tpu as pltpu

EPS = 1e-6
POOL_WINDOWS = (2, 4, 8, 16)
POOL_HALO = 16
HEAD_DIM = 128
CHUNK = 64
NORM_ROWS = 256
PROJ_SLAB_ROWS = 512
OUTPROJ_SLAB_ROWS = 128
HGRN_HEADS_PER_STEP = 16
HGRN_CHUNKS_PER_ITER = 4
LEVELS = (32, 16, 8, 4, 2)

LOG2_E = 1.4426950408889634

F32 = jnp.float32
BF16 = jnp.bfloat16

VMEM_LIMIT_BYTES = 60 * 1024 * 1024


def _tile(dim, pref, align=128):
    if dim <= pref:
        return dim
    t = (pref // align) * align
    while t >= align:
        if dim % t == 0:
            return t
        t -= align
    return dim


def _params(sem):
    return pltpu.CompilerParams(dimension_semantics=sem, vmem_limit_bytes=VMEM_LIMIT_BYTES)


def _slabs(rows, slab):
    n = min(rows, slab)
    assert rows % n == 0
    return [slice(r, r + n) for r in range(0, rows, n)]


def _sigmoid(x):
    return 1.0 / (1.0 + jnp.exp(-x))


def _rms_scale(h):
    return lax.rsqrt(jnp.mean(h * h, axis=-1, keepdims=True) + EPS)


def _rms_scale_ref(ref, r, n):
    half = ref.shape[1] // 2
    lo = ref[r:r + n, :half]
    hi = ref[r:r + n, half:]
    ss = jnp.sum(lo * lo, axis=-1, keepdims=True) + jnp.sum(hi * hi, axis=-1, keepdims=True)
    return lax.rsqrt(ss / ref.shape[1] + EPS)


def _proj_kernel(x_hbm, g_ref, w_ref, lbp_ref, o_ref, u_ref, xbuf, sem, *, layer, tn, half):
    i = pl.program_id(0)
    tm = u_ref.shape[0]
    rc = xbuf.shape[1]

    def xcopy(c):
        src = x_hbm.at[pl.ds(pl.multiple_of(i * tm, tm) + c * rc, rc), :]
        return pltpu.make_async_copy(src, xbuf.at[c % 2], sem.at[c % 2])

    @pl.when(pl.program_id(1) == 0)
    def _():
        nchunks = tm // rc
        for c in range(min(2, nchunks)):
            xcopy(c).start()
        for c in range(nchunks):
            xcopy(c).wait()
            slot = xbuf.at[c % 2]
            scale = _rms_scale_ref(slot, 0, rc)
            u_ref[c * rc:(c + 1) * rc, :] = (slot[...] * scale * g_ref[...]).astype(BF16)
            if c + 2 < nchunks:
                xcopy(c + 2).start()

    seg = (pl.program_id(1) * tn) // half
    p = lbp_ref[...]
    e = jnp.exp(p - jnp.max(p, axis=0, keepdims=True))
    lb = jnp.sum(e[: layer + 1], axis=0, keepdims=True) / jnp.sum(e, axis=0, keepdims=True)
    is_f = seg == 2
    is_silu = jnp.logical_or(seg == 1, seg == 4)
    c0 = jnp.where(is_f, lb, 0.0)
    c1 = jnp.where(jnp.logical_or(seg == 0, seg == 3), 1.0, 0.0)
    c2 = jnp.where(is_f, 1.0 - lb, jnp.where(seg >= 5, 1.0, 0.0))
    c3 = jnp.where(is_silu, 1.0, 0.0)
    w = w_ref[...].astype(BF16)
    for rows in _slabs(tm, PROJ_SLAB_ROWS):
        acc = jnp.dot(u_ref[rows, :], w, preferred_element_type=F32)
        o_ref[rows, :] = c0 + c1 * acc + _sigmoid(acc) * (c2 + c3 * acc)


def _proj(x, g, w, lb_param, layer):
    t, d = x.shape
    n_in = w.shape[1]
    half = d // 2
    tm = _tile(t, 2048, 8)
    tn = _tile(half, 512)
    rc = _tile(tm, NORM_ROWS, 8)
    nb = half // tn
    return pl.pallas_call(
        functools.partial(_proj_kernel, layer=layer, tn=tn, half=half),
        out_shape=jax.ShapeDtypeStruct((t, n_in), F32),
        grid=(t // tm, n_in // tn),
        in_specs=[pl.BlockSpec(memory_space=pl.ANY),
                  pl.BlockSpec((1, d), lambda i, j: (0, 0)),
                  pl.BlockSpec((d, tn), lambda i, j: (0, j)),
                  pl.BlockSpec((lb_param.shape[0], tn), lambda i, j: (0, jnp.clip(j - 2 * nb, 0, nb - 1)))],
        out_specs=pl.BlockSpec((tm, tn), lambda i, j: (i, j)),
        scratch_shapes=[pltpu.VMEM((tm, d), BF16), pltpu.VMEM((2, rc, d), F32), pltpu.SemaphoreType.DMA((2,))],
        compiler_params=_params(("arbitrary", "arbitrary")),
        name="in_proj",
    )(x, g.reshape(1, d), w, lb_param)


def _pool_kernel(z_ref, halo_ref, w_ref, s_ref, o_ref, *, tp):
    i = pl.program_id(0)
    g = pl.program_id(1)
    z = z_ref[...]
    halo = jnp.where(i > 0, halo_ref[...], 0.0)
    full = jnp.concatenate([halo, z], axis=0)
    t = i * tp + lax.broadcasted_iota(jnp.int32, (tp, 1), 0)

    for gi, w in enumerate(POOL_WINDOWS):
        @pl.when(g == gi)
        def _(w=w):
            s = full
            k = 1
            while k < w:
                s = s + pltpu.roll(s, k, 0)
                k *= 2
            count = jnp.minimum(t + 1, w).astype(F32)
            pooled = s[POOL_HALO:] / count - z
            y = jnp.dot(pooled.astype(BF16), w_ref[0].astype(BF16), preferred_element_type=F32)
            o_ref[...] = (y * s_ref[0]).astype(o_ref.dtype)


def _pool(act, w_group, scale):
    t = act.shape[0]
    ng, gsz, _ = w_group.shape
    assert ng == len(POOL_WINDOWS)
    tp = _tile(t, 2048, POOL_HALO)
    hb = tp // POOL_HALO
    return pl.pallas_call(
        functools.partial(_pool_kernel, tp=tp),
        out_shape=jax.ShapeDtypeStruct((t, ng * gsz), BF16),
        grid=(t // tp, ng),
        in_specs=[pl.BlockSpec((tp, gsz), lambda i, g: (i, g)),
                  pl.BlockSpec((POOL_HALO, gsz), lambda i, g: (jnp.maximum(i * hb - 1, 0), g)),
                  pl.BlockSpec((1, gsz, gsz), lambda i, g: (g, 0, 0)),
                  pl.BlockSpec((1, 1, gsz), lambda i, g: (g, 0, 0))],
        out_specs=pl.BlockSpec((tp, gsz), lambda i, g: (i, g)),
        compiler_params=_params(("parallel", "arbitrary")),
        name="pool_mixer",
    )(act, act, w_group, scale.reshape(ng, 1, gsz))


def _level_ids():
    t = np.arange(CHUNK)[:, None]
    s = np.arange(CHUNK)[None, :]
    ids = -np.ones((CHUNK, CHUNK), np.int32)
    ids[t == s] = 0
    for lid, n in enumerate((1,) + tuple(reversed(LEVELS)), start=1):
        m = (t // (2 * n) == s // (2 * n)) & (t % (2 * n) >= n) & (s % (2 * n) < n)
        ids[m] = lid
    return ids


def _nt_dot(a, b):
    return lax.dot_general(a, b, (((1,), (1,)), ((), ())), preferred_element_type=F32)


def _tn_dot(a, b):
    return lax.dot_general(a, b, (((0,), (0,)), ((), ())), preferred_element_type=F32)


def _hgrn_kernel(q_ref, f_ref, v_ref, og_ref, gn_ref, ids_ref, o_ref, st_ref, b_ref, *, nh, nchunk):
    @pl.when(pl.program_id(1) == 0)
    def _():
        st_ref[...] = jnp.zeros_like(st_ref)

    ids = ids_ref[...]
    sub = lax.broadcasted_iota(jnp.int32, (8, HEAD_DIM), 0)
    row = lax.broadcasted_iota(jnp.int32, (CHUNK, HEAD_DIM), 0)
    tril = (lax.broadcasted_iota(jnp.int32, (CHUNK, CHUNK), 0)
            >= lax.broadcasted_iota(jnp.int32, (CHUNK, CHUNK), 1)).astype(BF16)

    def bcast_row(h, r, n):
        return jnp.broadcast_to(b_ref[h, r:r + 1, :], (n, HEAD_DIM))

    def neg_abs(x):
        return pltpu.bitcast(pltpu.bitcast(x, jnp.uint32) | jnp.uint32(0x80000000), F32)

    def chunk_body(c, carry):
        heads = range(nh * HGRN_CHUNKS_PER_ITER)
        starts = [pl.multiple_of((c * HGRN_CHUNKS_PER_ITER + h // nh) * CHUNK, CHUNK) for h in heads]
        rows = [pl.ds(starts[h], CHUNK) for h in heads]
        lanes = [slice((h % nh) * HEAD_DIM, (h % nh + 1) * HEAD_DIM) for h in heads]
        def qv(h, a=0, n=CHUNK):
            return q_ref[pl.ds(starts[h] + a, n), lanes[h]]

        def fv(h, a=0, n=CHUNK):
            return f_ref[pl.ds(starts[h] + a, n), lanes[h]]

        def kv(h, a=0, n=CHUNK):
            return 1.0 - fv(h, a, n)

        def bv(h, a=0, n=CHUNK):
            return b_ref[h, a:a + n, :]

        scores = {}
        out = {}

        def stage_decay(h):
            g = jnp.log(fv(h)) * LOG2_E
            g_hi = g.astype(BF16)
            g_r = g - g_hi.astype(F32)
            g_mid = g_r.astype(BF16)
            g_lo = (g_r - g_mid.astype(F32)).astype(BF16)
            parts = jnp.dot(tril, jnp.concatenate([g_hi, g_mid, g_lo], axis=1), preferred_element_type=F32)
            b_ref[h] = (parts[:, :HEAD_DIM] + parts[:, HEAD_DIM:2 * HEAD_DIM]) + parts[:, 2 * HEAD_DIM:]

        def stage_diagonal(h):
            k_bf = kv(h).astype(BF16)
            s = jnp.where(ids == 0, _nt_dot(qv(h).astype(BF16), k_bf), 0.0)
            scores[h] = jnp.where(ids == 1, _nt_dot((qv(h) * fv(h)).astype(BF16), k_bf), s)

        def stage_level(h, lid, n):
            if n >= 8:
                d_parts, x_parts = [], []
                for base in range(0, CHUNK, 2 * n):
                    mid = bcast_row(h, base + n - 1, n)
                    d_parts += [mid - bv(h, base, n), bv(h, base + n, n) - mid]
                    x_parts += [kv(h, base, n), qv(h, base + n, n)]
                d = jnp.concatenate(d_parts, axis=0)
                xs = jnp.concatenate(x_parts, axis=0)
            else:
                if n == 4:
                    parts = [bcast_row(h, 8 * i + 3, 8) for i in range(CHUNK // 8)]
                else:
                    parts = [jnp.where(sub < 4, bcast_row(h, 8 * i + 1, 8), bcast_row(h, 8 * i + 5, 8))
                             for i in range(CHUNK // 8)]
                d = neg_abs(bv(h) - jnp.concatenate(parts, axis=0))
                xs = jnp.where((row & n) != 0, qv(h), kv(h))
            x = (xs * jnp.exp2(d)).astype(BF16)
            scores[h] = jnp.where(ids == lid, _nt_dot(x, x), scores[h])

        def stage_state(h):
            b_end = bcast_row(h, CHUNK - 1, CHUNK)
            st = st_ref[h % nh]
            vh = v_ref[rows[h], lanes[h]].astype(BF16)
            oh = jnp.dot(scores.pop(h).astype(BF16), vh, preferred_element_type=F32)
            out[h] = oh + _nt_dot((qv(h) * jnp.exp2(bv(h))).astype(BF16), st.astype(BF16))
            k_dec = (kv(h) * jnp.exp2(b_end - bv(h))).astype(BF16)
            st_ref[h % nh] = st * jnp.exp2(b_end[:1]) + _tn_dot(vh, k_dec)

        def stage_output(h):
            o = out.pop(h)
            y = o * _rms_scale(o) * gn_ref[:, lanes[h]] * og_ref[rows[h], lanes[h]]
            o_ref[rows[h], lanes[h]] = y.astype(o_ref.dtype)

        stages = [stage_decay, stage_diagonal]
        stages += [functools.partial(stage_level, lid=lid, n=n) for lid, n in enumerate(reversed(LEVELS), start=2)]
        stages += [stage_state, stage_output]
        for stage in stages:
            for h in heads:
                stage(h)
        return carry

    lax.fori_loop(0, nchunk // HGRN_CHUNKS_PER_ITER, chunk_body, 0)


def _hgrn(act, gnorm, d):
    t = act.shape[0]
    width = d // 2
    nh = min(HGRN_HEADS_PER_STEP, width // HEAD_DIM)
    bw = nh * HEAD_DIM
    ngroups = width // bw
    tr = _tile(t, 512, CHUNK * HGRN_CHUNKS_PER_ITER)
    assert tr % (CHUNK * HGRN_CHUNKS_PER_ITER) == 0

    def seg_spec(seg):
        return pl.BlockSpec((tr, bw), lambda g, r, seg=seg: (r, seg * ngroups + g))

    return pl.pallas_call(
        functools.partial(_hgrn_kernel, nh=nh, nchunk=tr // CHUNK),
        out_shape=jax.ShapeDtypeStruct((t, width), BF16),
        grid=(ngroups, t // tr),
        in_specs=[seg_spec(1), seg_spec(2), seg_spec(3), seg_spec(4),
                  pl.BlockSpec((1, bw), lambda g, r: (0, g)),
                  pl.BlockSpec((CHUNK, CHUNK), lambda g, r: (0, 0))],
        out_specs=pl.BlockSpec((tr, bw), lambda g, r: (r, g)),
        scratch_shapes=[pltpu.VMEM((nh, HEAD_DIM, HEAD_DIM), F32),
                        pltpu.VMEM((nh * HGRN_CHUNKS_PER_ITER, CHUNK, HEAD_DIM), F32)],
        compiler_params=_params(("parallel", "arbitrary")),
        name="hgrn2",
    )(act, act, act, act, gnorm.reshape(1, width), jnp.asarray(_level_ids()))


def _merge_kernel(yp_ref, yh_ref, wp_ref, wh_ref, ga_ref, gb_ref, o_ref):
    a = jnp.dot(yp_ref[...], wp_ref[...].astype(BF16), preferred_element_type=F32)
    b = jnp.dot(yh_ref[...], wh_ref[...].astype(BF16), preferred_element_type=F32)
    o_ref[...] = (ga_ref[...] * a + gb_ref[...] * b).astype(o_ref.dtype)


def _merge(y_pool, y_hgrn, w_up_pool, w_up_hgrn, act, d):
    t, kp = y_pool.shape
    kh = y_hgrn.shape[1]
    tm = _tile(t, 2048, 8)
    tn = _tile(d // 2, 256)
    nb = d // tn
    off_a = (act.shape[1] - 2 * d) // tn
    return pl.pallas_call(
        _merge_kernel,
        out_shape=jax.ShapeDtypeStruct((t, d), BF16),
        grid=(t // tm, nb),
        in_specs=[pl.BlockSpec((tm, kp), lambda i, j: (i, 0)),
                  pl.BlockSpec((tm, kh), lambda i, j: (i, 0)),
                  pl.BlockSpec((kp, tn), lambda i, j: (0, j)),
                  pl.BlockSpec((kh, tn), lambda i, j: (0, j)),
                  pl.BlockSpec((tm, tn), lambda i, j: (i, off_a + j)),
                  pl.BlockSpec((tm, tn), lambda i, j: (i, off_a + nb + j))],
        out_specs=pl.BlockSpec((tm, tn), lambda i, j: (i, j)),
        compiler_params=_params(("parallel", "arbitrary")),
        name="gated_merge",
    )(y_pool, y_hgrn, w_up_pool, w_up_hgrn, act, act)


def _outproj_kernel(m_ref, w_ref, x_ref, h_ref):
    w = w_ref[...].astype(BF16)
    for rows in _slabs(h_ref.shape[0], OUTPROJ_SLAB_ROWS):
        h_ref[rows, :] = x_ref[rows, :] + jnp.dot(m_ref[rows, :], w, preferred_element_type=F32)


def _outproj(m, w, x):
    t, d = x.shape
    tm = _tile(t, 2048, 8)
    tn = _tile(d, 256)
    return pl.pallas_call(
        _outproj_kernel,
        out_shape=jax.ShapeDtypeStruct((t, d), F32),
        grid=(t // tm, d // tn),
        in_specs=[pl.BlockSpec((tm, d), lambda i, j: (i, 0)),
                  pl.BlockSpec((d, tn), lambda i, j: (0, j)),
                  pl.BlockSpec((tm, tn), lambda i, j: (i, j))],
        out_specs=pl.BlockSpec((tm, tn), lambda i, j: (i, j)),
        compiler_params=_params(("parallel", "arbitrary")),
        name="out_proj",
    )(m, w, x)


def _ffn_kernel(h_hbm, wg_ref, wu_ref, wd_ref, gin_ref, gout_ref, o_hbm, acc_ref, v_ref, sem,
                *, nj, tm, final_norm):
    i = pl.program_id(0)
    j = pl.program_id(1)
    chunk = min(tm, NORM_ROWS)
    starts = range(0, tm, chunk)

    def copy(c, r, to_vmem):
        hbm = (h_hbm if to_vmem else o_hbm).at[pl.ds(pl.multiple_of(i * tm, tm) + r, chunk), :]
        vmem = acc_ref.at[r:r + chunk, :]
        return pltpu.make_async_copy(hbm, vmem, sem.at[c]) if to_vmem else \
            pltpu.make_async_copy(vmem, hbm, sem.at[c])

    @pl.when(j == 0)
    def _():
        for c, r in enumerate(starts):
            copy(c, r, True).start()
        for c, r in enumerate(starts):
            copy(c, r, True).wait()
            scale = _rms_scale_ref(acc_ref, r, chunk)
            v_ref[r:r + chunk, :] = (acc_ref[r:r + chunk, :] * scale * gin_ref[...]).astype(BF16)

    v = v_ref[...]
    gate = jnp.dot(v, wg_ref[...].astype(BF16), preferred_element_type=F32)
    up = jnp.dot(v, wu_ref[...].astype(BF16), preferred_element_type=F32)
    a = (gate * _sigmoid(gate) * up).astype(BF16)
    acc_ref[...] += jnp.dot(a, wd_ref[...].astype(BF16), preferred_element_type=F32)

    @pl.when(j == nj - 1)
    def _():
        for c, r in enumerate(starts):
            if final_norm:
                scale = _rms_scale_ref(acc_ref, r, chunk)
                acc_ref[r:r + chunk, :] = acc_ref[r:r + chunk, :] * scale * gout_ref[...]
            copy(c, r, False).start()
        for c, r in enumerate(starts):
            copy(c, r, False).wait()


def _ffn(h, wg, wu, wd, g_in, g_out, final_norm):
    t, d = h.shape
    ff = wd.shape[0]
    tm = _tile(t, 1024, 8)
    tf = _tile(ff, 256)
    nj = ff // tf
    return pl.pallas_call(
        functools.partial(_ffn_kernel, nj=nj, tm=tm, final_norm=final_norm),
        out_shape=jax.ShapeDtypeStruct((t, d), F32),
        grid=(t // tm, nj),
        in_specs=[pl.BlockSpec(memory_space=pl.ANY),
                  pl.BlockSpec((d, tf), lambda i, j: (0, j)),
                  pl.BlockSpec((d, tf), lambda i, j: (0, j)),
                  pl.BlockSpec((tf, d), lambda i, j: (j, 0)),
                  pl.BlockSpec((1, d), lambda i, j: (0, 0)),
                  pl.BlockSpec((1, d), lambda i, j: (0, 0))],
        out_specs=pl.BlockSpec(memory_space=pl.ANY),
        scratch_shapes=[pltpu.VMEM((tm, d), F32), pltpu.VMEM((tm, d), BF16),
                        pltpu.SemaphoreType.DMA((pl.cdiv(tm, min(tm, NORM_ROWS)),))],
        compiler_params=_params(("arbitrary", "arbitrary")),
        name="swiglu_ffn",
    )(h, wg, wu, wd, g_in.reshape(1, d), g_out.reshape(1, d))


def kernel(x, g_mix, w_in, w_pool_group, pool_scale, lb_param, hgrn_norm, w_up_pool, w_up_hgrn, w_out,
           g_ffn, w_ffn_gate, w_ffn_up, w_ffn_down, g_final):
    bsz, t, d = x.shape
    depth = g_mix.shape[0]
    assert d % (2 * HEAD_DIM) == 0 and t % CHUNK == 0
    outs = []
    for bi in range(bsz):
        h = x[bi]
        for l in range(depth):
            act = _proj(h, g_mix[l], w_in[l], lb_param, l)
            y_pool = _pool(act, w_pool_group[l], pool_scale[l])
            y_hgrn = _hgrn(act, hgrn_norm[l], d)
            merged = _merge(y_pool, y_hgrn, w_up_pool[l], w_up_hgrn[l], act, d)
            h1 = _outproj(merged, w_out[l], h)
            h = _ffn(h1, w_ffn_gate[l], w_ffn_up[l], w_ffn_down[l], g_ffn[l], g_final, l == depth - 1)
        outs.append(h)
    return jnp.stack(outs, axis=0)
```
